```python
import jax, jax.numpy as jnp
from jax import lax
import numpy as np

D_MODEL = 4096
BATCH = 4
SEQ = 2048
DEPTH = 2
DEC_BATCH = 32
DEC_SEQ = 1
PAST_LEN = 16384
PAGE_SIZE = 128

N_MIXERS = 2
N_DN_LAYERS = (DEPTH + 1) // 2
N_SWA_LAYERS = DEPTH // 2
NORM_EPS = 1e-6

DN_DK = 128
DN_DV = 128
DN_HK = D_MODEL // DN_DK
DN_HV = 2 * DN_HK
DN_QK_DIM = DN_HK * DN_DK
DN_V_DIM = DN_HV * DN_DV
DN_CONV_CH = 2 * DN_QK_DIM + DN_V_DIM
DN_IN = DN_CONV_CH + DN_V_DIM + 2 * DN_HV
DN_CONV_W = 4
DN_CHUNK = 64

SWA_HD = 64
SWA_HEADS = D_MODEL // SWA_HD
SWA_KV = SWA_HEADS // 8
SWA_GROUP = SWA_HEADS // SWA_KV
SWA_WINDOW = 128
SWA_BLOCK = SWA_WINDOW
SWA_IN = (SWA_HEADS + 2 * SWA_KV) * SWA_HD
ROPE_THETA = 10000.0

N_EXPERTS = 64
MOE_TOP_K = 8
MOE_GROUPS = 8
MOE_TOPK_GROUPS = 4
D_EXPERT = D_MODEL // 4
D_SHARED = D_MODEL // 4
ROUTED_SCALE = 2.5

kernel_name = 'hybrid_deltanet_swa_moe_step'


def rmsnorm(x, g):
    xf = x.astype(jnp.float32)
    y = xf * lax.rsqrt(jnp.mean(xf * xf, axis=-1, keepdims=True) + NORM_EPS)
    return (y * g.astype(jnp.float32)).astype(x.dtype)


def l2norm(x):
    xf = x.astype(jnp.float32)
    return xf * lax.rsqrt(jnp.sum(xf * xf, axis=-1, keepdims=True) + NORM_EPS)


def rope(x, pos):
    half = x.shape[-1] // 2
    inv = ROPE_THETA ** (-jnp.arange(half, dtype=jnp.float32) / half)
    ang = pos.astype(jnp.float32)[:, None] * inv[None, :]
    cos = jnp.cos(ang)[:, None, :]
    sin = jnp.sin(ang)[:, None, :]
    xf = x.astype(jnp.float32)
    x1, x2 = xf[..., :half], xf[..., half:]
    return jnp.concatenate([x1 * cos - x2 * sin, x2 * cos + x1 * sin], axis=-1).astype(x.dtype)


def gated_delta_rule(q, k, v, g, beta, s0):
    b, L, H, dk = q.shape
    dv = v.shape[-1]
    C = DN_CHUNK
    pad = (-L) % C
    n = (L + pad) // C

    def chunks(x):
        x = jnp.pad(x.astype(jnp.float32), [(0, 0), (0, pad)] + [(0, 0)] * (x.ndim - 2))
        x = x.reshape((b, n, C) + x.shape[2:])
        return jnp.moveaxis(x, (1, 2), (0, 3))

    q = chunks(q) * dk ** -0.5
    k = chunks(k)
    v = chunks(v)
    g = chunks(g)
    beta = chunks(beta)
    gc = jnp.cumsum(g, axis=-1)
    causal = jnp.tril(jnp.ones((C, C), dtype=bool))
    strict = jnp.tril(jnp.ones((C, C), dtype=bool), -1)
    decay = jnp.exp(jnp.where(causal, gc[..., :, None] - gc[..., None, :], -jnp.inf))
    kb = k * beta[..., None]
    lmat = jnp.where(strict, jnp.einsum('nbhcd,nbhsd->nbhcs', kb, k) * decay, 0.0)
    a_mat = lmat + jnp.eye(C, dtype=jnp.float32)
    rhs = jnp.concatenate([v * beta[..., None], kb * jnp.exp(gc)[..., None]], axis=-1)
    sol = lax.linalg.triangular_solve(a_mat, rhs, left_side=True, lower=True, unit_diagonal=True)
    u, w = sol[..., :dv], sol[..., dv:]
    attn = jnp.einsum('nbhcd,nbhsd->nbhcs', q, k) * decay

    def step(S, xs):
        q_c, k_c, u_c, w_c, gc_c, attn_c = xs
        v_new = u_c - jnp.einsum('bhcd,bhde->bhce', w_c, S)
        o = jnp.einsum('bhcd,bhde->bhce', q_c * jnp.exp(gc_c)[..., None], S) + jnp.einsum('bhcs,bhse->bhce', attn_c, v_new)
        g_last = gc_c[..., -1]
        S = S * jnp.exp(g_last)[..., None, None] + jnp.einsum('bhcd,bhce->bhde', k_c * jnp.exp(g_last[..., None] - gc_c)[..., None], v_new)
        return S, o

    s_fin, o = lax.scan(step, s0.astype(jnp.float32), (q, k, u, w, gc, attn))
    o = jnp.moveaxis(o, (0, 3), (1, 2)).reshape(b, n * C, H, dv)[:, :L]
    return o, s_fin


def gated_deltanet(h, conv_prev, s_prev, w_in, conv_w, a_log, dt_bias, norm_g, w_out):
    b, t, _ = h.shape
    proj = h @ w_in
    qkv = proj[..., :DN_CONV_CH]
    z = proj[..., DN_CONV_CH:DN_CONV_CH + DN_V_DIM]
    beta_in = proj[..., DN_CONV_CH + DN_V_DIM:DN_CONV_CH + DN_V_DIM + DN_HV]
    a_in = proj[..., DN_CONV_CH + DN_V_DIM + DN_HV:]
    xp = jnp.concatenate([conv_prev.astype(qkv.dtype), qkv], axis=1)
    conv = jax.nn.silu(sum(xp[:, j:j + t] * conv_w[j] for j in range(DN_CONV_W)))
    new_conv = xp[:, t:]
    q = l2norm(conv[..., :DN_QK_DIM].reshape(b, t, DN_HK, DN_DK))
    k = l2norm(conv[..., DN_QK_DIM:2 * DN_QK_DIM].reshape(b, t, DN_HK, DN_DK))
    v = conv[..., 2 * DN_QK_DIM:].reshape(b, t, DN_HV, DN_DV)
    rep = DN_HV // DN_HK
    q = jnp.repeat(q, rep, axis=2)
    k = jnp.repeat(k, rep, axis=2)
    beta = jax.nn.sigmoid(beta_in.astype(jnp.float32))
    g = -jnp.exp(a_log.astype(jnp.float32)) * jax.nn.softplus(a_in.astype(jnp.float32) + dt_bias.astype(jnp.float32))
    o, s_new = gated_delta_rule(q, k, v, g, beta, s_prev)
    o = rmsnorm(o, norm_g) * jax.nn.silu(z.reshape(b, t, DN_HV, DN_DV).astype(jnp.float32))
    out = o.astype(h.dtype).reshape(b, t, DN_V_DIM) @ w_out
    return out, new_conv, s_new.astype(s_prev.dtype)


def swa_qkv(h, pos, w_in, b_in):
    b, t, _ = h.shape
    proj = h @ w_in + b_in
    nq = SWA_HEADS * SWA_HD
    nk = SWA_KV * SWA_HD
    q = rope(proj[..., :nq].reshape(b, t, SWA_HEADS, SWA_HD), pos)
    k = rope(proj[..., nq:nq + nk].reshape(b, t, SWA_KV, SWA_HD), pos)
    v = proj[..., nq + nk:].reshape(b, t, SWA_KV, SWA_HD)
    return q.reshape(b, t, SWA_KV, SWA_GROUP, SWA_HD), k, v


def attend_with_sinks(q, k, v, mask, sinks):
    s = jnp.einsum('btkgd,bskd->bkgts', q.astype(jnp.float32), k.astype(jnp.float32)) * SWA_HD ** -0.5
    s = jnp.where(mask, s, -jnp.inf)
    sink = sinks.astype(jnp.float32).reshape(SWA_KV, SWA_GROUP)[None, :, :, None, None]
    m = jnp.maximum(jnp.max(s, axis=-1, keepdims=True), sink)
    p = jnp.exp(s - m)
    p = p / (jnp.sum(p, axis=-1, keepdims=True) + jnp.exp(sink - m))
    return jnp.einsum('bkgts,bskd->btkgd', p.astype(v.dtype), v)


def swa_prompt(h, w_in, b_in, sinks, w_out, b_out):
    b, t, _ = h.shape
    q, k, v = swa_qkv(h, jnp.arange(t), w_in, b_in)
    nb = t // SWA_BLOCK

    def band(x):
        xp = jnp.concatenate([jnp.zeros_like(x[:, :SWA_BLOCK]), x], axis=1).reshape(b, nb + 1, SWA_BLOCK, SWA_KV, SWA_HD)
        return jnp.moveaxis(jnp.concatenate([xp[:, :-1], xp[:, 1:]], axis=2), 1, 0)

    qb = jnp.moveaxis(q.reshape(b, nb, SWA_BLOCK, SWA_KV, SWA_GROUP, SWA_HD), 1, 0)
    blk = jnp.arange(nb)[:, None, None]
    qpos = blk * SWA_BLOCK + jnp.arange(SWA_BLOCK)[None, :, None]
    kpos = (blk - 1) * SWA_BLOCK + jnp.arange(2 * SWA_BLOCK)[None, None, :]
    mask = (kpos <= qpos) & (kpos > qpos - SWA_WINDOW) & (kpos >= 0)
    o = lax.map(lambda xs: attend_with_sinks(xs[0], xs[1], xs[2], xs[3], sinks), (qb, band(k), band(v), mask))
    o = jnp.moveaxis(o, 0, 1).reshape(b, t, SWA_HEADS * SWA_HD)
    keep = min(SWA_WINDOW, t)
    return o @ w_out + b_out, k[:, t - keep:], v[:, t - keep:]


def swa_sample(h, k_cache, v_cache, w_in, b_in, sinks, w_out, b_out):
    b, t, _ = h.shape
    buf = k_cache.shape[1]
    qpos = PAST_LEN + jnp.arange(t)
    q, k, v = swa_qkv(h, qpos, w_in, b_in)
    k_all = jnp.concatenate([k_cache.astype(k.dtype), k], axis=1)
    v_all = jnp.concatenate([v_cache.astype(v.dtype), v], axis=1)
    kpos = PAST_LEN - buf + jnp.arange(buf + t)
    mask = (kpos[None, :] <= qpos[:, None]) & (kpos[None, :] > qpos[:, None] - SWA_WINDOW)
    o = attend_with_sinks(q, k_all, v_all, mask, sinks).reshape(b, t, SWA_HEADS * SWA_HD)
    return o @ w_out + b_out, k_all[:, t:], v_all[:, t:]


def moe_ffn(h, l, router_w, router_bias, w_in, w_out, sh_in, sh_out):
    b, t, d = h.shape
    n = b * t
    x = h.reshape(n, d)
    scores = jax.nn.sigmoid(x.astype(jnp.float32) @ router_w[l].astype(jnp.float32))
    sel = scores + router_bias[l].astype(jnp.float32)
    grp_score = lax.top_k(sel.reshape(n, MOE_GROUPS, N_EXPERTS // MOE_GROUPS), 2)[0].sum(-1)
    _, top_grp = lax.top_k(grp_score, MOE_TOPK_GROUPS)
    grp_mask = jax.nn.one_hot(top_grp, MOE_GROUPS, dtype=jnp.float32).sum(-2) > 0
    exp_mask = jnp.repeat(grp_mask, N_EXPERTS // MOE_GROUPS, axis=-1)
    _, top_e = lax.top_k(jnp.where(exp_mask, sel, -jnp.inf), MOE_TOP_K)
    wts = jnp.take_along_axis(scores, top_e, axis=-1)
    wts = wts / (jnp.sum(wts, axis=-1, keepdims=True) + 1e-20) * ROUTED_SCALE
    gates = jnp.einsum('nk,nke->ne', wts, jax.nn.one_hot(top_e, N_EXPERTS, dtype=jnp.float32)).astype(x.dtype)

    def expert(y, e):
        gu = x @ w_in[l, e]
        act = jax.nn.silu(gu[:, :D_EXPERT]) * gu[:, D_EXPERT:]
        return y + (act * lax.dynamic_index_in_dim(gates, e, axis=1)) @ w_out[l, e], None

    y, _ = lax.scan(expert, jnp.zeros_like(x), jnp.arange(N_EXPERTS))
    gu = x @ sh_in[l]
    y = y + (jax.nn.silu(gu[:, :D_SHARED]) * gu[:, D_SHARED:]) @ sh_out[l]
    return y.reshape(b, t, d)


def trunk(x, c, conv_in, rec_in, k_in, v_in, is_prompt, w):
    b, t, _ = x.shape
    conv_out, rec_out, k_out, v_out = [], [], [], []
    c_act = jax.nn.silu(c)
    for l in range(DEPTH):
        mod = (c_act @ w['ada_w'][l] + w['ada_b'][l])[:, None, :]
        sh_m, sc_m, gt_m, sh_f, sc_f, gt_f = jnp.split(mod, 6, axis=-1)
        hm = rmsnorm(x, w['norm_mix_g'][l]) * (1 + sc_m) + sh_m
        j = l // N_MIXERS
        if l % N_MIXERS == 0:
            if is_prompt:
                conv_prev = jnp.zeros((b, DN_CONV_W - 1, DN_CONV_CH), x.dtype)
                s_prev = jnp.zeros((b, DN_HV, DN_DK, DN_DV), x.dtype)
            else:
                conv_prev, s_prev = conv_in[j], rec_in[j]
            mix, cv, st = gated_deltanet(hm, conv_prev, s_prev, w['dn_w_in'][j], w['dn_conv_w'][j], w['dn_a_log'][j],
                                         w['dn_dt_bias'][j], w['dn_norm_g'][j], w['dn_w_out'][j])
            conv_out.append(cv)
            rec_out.append(st)
        else:
            if is_prompt:
                mix, kk, vv = swa_prompt(hm, w['swa_w_in'][j], w['swa_b_in'][j], w['swa_sinks'][j], w['swa_w_out'][j], w['swa_b_out'][j])
            else:
                mix, kk, vv = swa_sample(hm, k_in[j], v_in[j], w['swa_w_in'][j], w['swa_b_in'][j], w['swa_sinks'][j],
                                         w['swa_w_out'][j], w['swa_b_out'][j])
            k_out.append(kk)
            v_out.append(vv)
        x = x + gt_m * mix
        hf = rmsnorm(x, w['norm_ffn_g'][l]) * (1 + sc_f) + sh_f
        x = x + gt_f * moe_ffn(hf, l, w['moe_router_w'], w['moe_router_bias'], w['moe_w_in'], w['moe_w_out'], w['moe_sh_in'], w['moe_sh_out'])
    y = rmsnorm(x, w['final_norm_g'])
    return y, jnp.stack(conv_out), jnp.stack(rec_out), jnp.stack(k_out), jnp.stack(v_out)


def setup_inputs(seed: int = 0) -> dict:
    key = jax.random.key(seed)
    ks = jax.random.split(key, 32)
    f32 = jnp.float32

    def nrm(i, shape, scale):
        return jax.random.normal(ks[i], shape, f32) * scale

    win_buf = min(SWA_WINDOW, PAST_LEN)
    dt = jax.random.uniform(ks[16], (N_DN_LAYERS, DN_HV), f32, minval=0.001, maxval=0.1)
    return {
        'x_prompt': nrm(0, (BATCH, SEQ, D_MODEL), 1.0),
        'x_sample': nrm(1, (DEC_BATCH, DEC_SEQ, D_MODEL), 1.0),
        'state_dn_conv': nrm(2, (N_DN_LAYERS, DEC_BATCH, DN_CONV_W - 1, DN_CONV_CH), 1.0),
        'state_dn_rec': nrm(3, (N_DN_LAYERS, DEC_BATCH, DN_HV, DN_DK, DN_DV), 0.5),
        'cache_swa_k': nrm(4, (N_SWA_LAYERS, DEC_BATCH, win_buf, SWA_KV, SWA_HD), 1.0),
        'cache_swa_v': nrm(5, (N_SWA_LAYERS, DEC_BATCH, win_buf, SWA_KV, SWA_HD), 1.0),
        'c_prompt': nrm(6, (BATCH, D_MODEL), 1.0),
        'c_sample': nrm(7, (DEC_BATCH, D_MODEL), 1.0),
        'ada_w': nrm(8, (DEPTH, D_MODEL, 6 * D_MODEL), 0.5 * D_MODEL ** -0.5),
        'ada_b': nrm(9, (DEPTH, 6 * D_MODEL), 0.02),
        'norm_mix_g': 1.0 + nrm(10, (DEPTH, D_MODEL), 0.1),
        'norm_ffn_g': 1.0 + nrm(11, (DEPTH, D_MODEL), 0.1),
        'final_norm_g': 1.0 + nrm(12, (D_MODEL,), 0.1),
        'dn_w_in': nrm(13, (N_DN_LAYERS, D_MODEL, DN_IN), D_MODEL ** -0.5),
        'dn_conv_w': nrm(14, (N_DN_LAYERS, DN_CONV_W, DN_CONV_CH), DN_CONV_W ** -0.5),
        'dn_a_log': jnp.log(jax.random.uniform(ks[15], (N_DN_LAYERS, DN_HV), f32, minval=1.0, maxval=16.0)),
        'dn_dt_bias': dt + jnp.log(-jnp.expm1(-dt)),
        'dn_norm_g': 1.0 + nrm(17, (N_DN_LAYERS, DN_DV), 0.1),
        'dn_w_out': nrm(18, (N_DN_LAYERS, DN_V_DIM, D_MODEL), DN_V_DIM ** -0.5),
        'swa_w_in': nrm(19, (N_SWA_LAYERS, D_MODEL, SWA_IN), D_MODEL ** -0.5),
        'swa_b_in': nrm(20, (N_SWA_LAYERS, SWA_IN), 0.02),
        'swa_sinks': nrm(21, (N_SWA_LAYERS, SWA_HEADS), 1.0),
        'swa_w_out': nrm(22, (N_SWA_LAYERS, SWA_HEADS * SWA_HD, D_MODEL), (SWA_HEADS * SWA_HD) ** -0.5),
        'swa_b_out': nrm(23, (N_SWA_LAYERS, D_MODEL), 0.02),
        'moe_router_w': nrm(24, (DEPTH, D_MODEL, N_EXPERTS), D_MODEL ** -0.5),
        'moe_router_bias': nrm(25, (DEPTH, N_EXPERTS), 0.01),
        'moe_w_in': nrm(26, (DEPTH, N_EXPERTS, D_MODEL, 2 * D_EXPERT), D_MODEL ** -0.5),
        'moe_w_out': nrm(27, (DEPTH, N_EXPERTS, D_EXPERT, D_MODEL), D_EXPERT ** -0.5),
        'moe_sh_in': nrm(28, (DEPTH, D_MODEL, 2 * D_SHARED), D_MODEL ** -0.5),
        'moe_sh_out': nrm(29, (DEPTH, D_SHARED, D_MODEL), D_SHARED ** -0.5),
    }


def reference(x_prompt, x_sample, state_dn_conv, state_dn_rec, cache_swa_k, cache_swa_v, c_prompt, c_sample,
              ada_w, ada_b, norm_mix_g, norm_ffn_g, final_norm_g,
              dn_w_in, dn_conv_w, dn_a_log, dn_dt_bias, dn_norm_g, dn_w_out,
              swa_w_in, swa_b_in, swa_sinks, swa_w_out, swa_b_out,
              moe_router_w, moe_router_bias, moe_w_in, moe_w_out, moe_sh_in, moe_sh_out):
    w = dict(ada_w=ada_w, ada_b=ada_b, norm_mix_g=norm_mix_g, norm_ffn_g=norm_ffn_g, final_norm_g=final_norm_g,
             dn_w_in=dn_w_in, dn_conv_w=dn_conv_w, dn_a_log=dn_a_log, dn_dt_bias=dn_dt_bias, dn_norm_g=dn_norm_g,
             dn_w_out=dn_w_out, swa_w_in=swa_w_in, swa_b_in=swa_b_in, swa_sinks=swa_sinks, swa_w_out=swa_w_out,
             swa_b_out=swa_b_out, moe_router_w=moe_router_w, moe_router_bias=moe_router_bias, moe_w_in=moe_w_in,
             moe_w_out=moe_w_out, moe_sh_in=moe_sh_in, moe_sh_out=moe_sh_out)
    y_prompt, p_conv, p_rec, p_k, p_v = trunk(x_prompt, c_prompt, None, None, None, None, True, w)
    y_sample, s_conv, s_rec, s_k, s_v = trunk(x_sample, c_sample, state_dn_conv, state_dn_rec, cache_swa_k, cache_swa_v, False, w)
    return (y_prompt, y_sample, p_conv, p_rec, p_k, p_v, s_conv, s_rec, s_k, s_v)
```

```python
import functools

import jax
import jax.numpy as jnp
from jax import lax
from jax.experimental import pallas as pl
from jax.experimental.pallas import tpu as pltpu

F32 = jnp.float32
BF16 = jnp.bfloat16
I32 = jnp.int32

NORM_EPS = 1e-6
ROPE_THETA = 10000.0
ROUTED_SCALE = 2.5
MOE_GROUPS = 8
MOE_TOPK_GROUPS = 4
MOE_TOP_K = 8
SWA_WINDOW = 128
DN_CHUNK = 64
DN_SOLVE_BLOCK = 16

LANES = 128
SUBLANES = 8
VMEM_LIMIT = 56 * 1024 * 1024

ROW_TILE = 256
NORM_TILE = 128
MOE_TILE = 256
PAST_LEN = 16384
DN_HEADS_PER_STEP = 8


def _cparams(sem, vmem=None):
    return pltpu.CompilerParams(dimension_semantics=sem, vmem_limit_bytes=vmem)


def _round_up(x, m):
    return (x + m - 1) // m * m


def _pick_tile(n, pref, mult=SUBLANES):
    if n <= pref:
        return n
    for t in range(pref - pref % mult, 0, -mult):
        if n % t == 0:
            return t
    return n


def _silu(x):
    return x * jax.nn.sigmoid(x)


def _bdot(a, b):
    return jnp.dot(a.astype(BF16), b.astype(BF16), preferred_element_type=F32)


def _bdot_nt(a, b):
    return lax.dot_general(a.astype(BF16), b.astype(BF16), (((1,), (1,)), ((), ())), preferred_element_type=F32)


def _bdot_tn(a, b):
    return lax.dot_general(a.astype(BF16), b.astype(BF16), (((0,), (0,)), ((), ())), preferred_element_type=F32)


def _split3(x):
    h1 = x.astype(BF16)
    r1 = x - h1.astype(F32)
    h2 = r1.astype(BF16)
    h3 = (r1 - h2.astype(F32)).astype(BF16)
    return h1, h2, h3


def _mm_kernel(a_ref, w_ref, *rest, lhs_silu, has_bias):
    if has_bias:
        b_ref, o_ref, wc_ref = rest
    else:
        o_ref, wc_ref = rest

    @pl.when(pl.program_id(1) == 0)
    def _():
        wc_ref[...] = w_ref[...].astype(BF16)

    a = a_ref[...]
    if lhs_silu:
        a = _silu(a.astype(F32))
    acc = jnp.dot(a.astype(BF16), wc_ref[...], preferred_element_type=F32)
    if has_bias:
        acc = acc + b_ref[...]
    o_ref[...] = acc.astype(o_ref.dtype)


def _mm(a, w, l, bias=None, *, out_dtype=F32, lhs_silu=False, tm_pref=768, name):
    m, k = a.shape
    n = w.shape[2]
    tm = _pick_tile(m, tm_pref, 16)
    tn = min(n, 512 if k <= 4096 else 256)
    grid = (pl.cdiv(n, tn), m // tm)
    in_specs = [
        pl.BlockSpec((tm, k), lambda j, i: (i, 0)),
        pl.BlockSpec((None, k, tn), lambda j, i: (l, 0, j)),
    ]
    args = [a, w]
    if bias is not None:
        in_specs.append(pl.BlockSpec((None, 1, tn), lambda j, i: (l, 0, j)))
        args.append(bias.reshape(bias.shape[0], 1, n))
    return pl.pallas_call(
        functools.partial(_mm_kernel, lhs_silu=lhs_silu, has_bias=bias is not None),
        grid=grid,
        in_specs=in_specs,
        out_specs=pl.BlockSpec((tm, tn), lambda j, i: (i, j)),
        out_shape=jax.ShapeDtypeStruct((m, n), out_dtype),
        scratch_shapes=[pltpu.VMEM((k, tn), BF16)],
        compiler_params=_cparams(("arbitrary", "arbitrary"), VMEM_LIMIT),
        name=name,
    )(*args)


def _glu_kernel(a_ref, wg_ref, wu_ref, o_ref, wgc_ref, wuc_ref):
    @pl.when(pl.program_id(1) == 0)
    def _():
        wgc_ref[...] = wg_ref[...].astype(BF16)
        wuc_ref[...] = wu_ref[...].astype(BF16)

    a = a_ref[...]
    g = jnp.dot(a, wgc_ref[...], preferred_element_type=F32)
    u = jnp.dot(a, wuc_ref[...], preferred_element_type=F32)
    o_ref[...] = (_silu(g) * u).astype(o_ref.dtype)


def _glu(a, w, l, *, name):
    m, k = a.shape
    h = w.shape[2] // 2
    tm = _pick_tile(m, 768, 16)
    tn = _pick_tile(h, 256, LANES)
    nj = h // tn
    return pl.pallas_call(
        _glu_kernel,
        grid=(nj, m // tm),
        in_specs=[
            pl.BlockSpec((tm, k), lambda j, i: (i, 0)),
            pl.BlockSpec((None, k, tn), lambda j, i: (l, 0, j)),
            pl.BlockSpec((None, k, tn), lambda j, i: (l, 0, nj + j)),
        ],
        out_specs=pl.BlockSpec((tm, tn), lambda j, i: (i, j)),
        out_shape=jax.ShapeDtypeStruct((m, h), BF16),
        scratch_shapes=[pltpu.VMEM((k, tn), BF16), pltpu.VMEM((k, tn), BF16)],
        compiler_params=_cparams(("arbitrary", "arbitrary"), VMEM_LIMIT),
        name=name,
    )(a, w, w)


MOD_SH_M, MOD_SC_M, MOD_GT_M, MOD_SH_F, MOD_SC_F, MOD_GT_F = range(6)


def _rownorm_kernel(*refs, n_prompt_tiles, has_resid, has_mod, out_x, out_slab, d):
    refs = list(refs)
    x_ref = refs.pop(0)
    if has_resid:
        y_ref, gtb_ref, gtr_ref = refs.pop(0), refs.pop(0), refs.pop(0)
    g_ref = refs.pop(0)
    if has_mod:
        scb_ref, shb_ref, scr_ref, shr_ref = refs.pop(0), refs.pop(0), refs.pop(0), refs.pop(0)
    outs = refs
    is_prompt = pl.program_id(0) < n_prompt_tiles

    def body(prompt):
        x = x_ref[...]
        if has_resid:
            gt = gtb_ref[...] if prompt else gtr_ref[...]
            x = x + gt * y_ref[...]
        o = list(outs)
        if out_x:
            o.pop(0)[...] = x
        h = x * lax.rsqrt(jnp.mean(x * x, axis=-1, keepdims=True) + NORM_EPS) * g_ref[...]
        if has_mod:
            sc = scb_ref[...] if prompt else scr_ref[...]
            sh = shb_ref[...] if prompt else shr_ref[...]
            h = h * (1.0 + sc) + sh
        h_ref = o.pop(0)
        h_ref[...] = h.astype(h_ref.dtype)
        if out_slab:
            slab_ref = o.pop(0)
            rows = x.shape[0]
            for s in range(d // LANES):
                slab_ref[pl.ds(s, rows, stride=d // LANES), :] = h[:, s * LANES:(s + 1) * LANES]

    @pl.when(is_prompt)
    def _():
        body(True)

    @pl.when(jnp.logical_not(is_prompt))
    def _():
        body(False)


def _rownorm(x, g, l, modp, modr, *, n_prompt_rows, rows_per_seq, resid=None, mod=None, out_x=False,
             h_dtype=BF16, out_slab=False, name):
    rows, d = x.shape
    tr = NORM_TILE
    n_ptiles = n_prompt_rows // tr
    tiles_per_seq = rows_per_seq // tr
    nb = modp.shape[0]

    def bspec(c):
        return pl.BlockSpec((None, None, 1, d), lambda i: (jnp.minimum(i // tiles_per_seq, nb - 1), c, 0, 0))

    def rspec(c):
        return pl.BlockSpec((None, tr, d), lambda i: (c, jnp.maximum(i - n_ptiles, 0), 0))

    row_spec = pl.BlockSpec((tr, d), lambda i: (i, 0))
    in_specs, args = [row_spec], [x]
    if resid is not None:
        y, gt_c = resid
        in_specs += [row_spec, bspec(gt_c), rspec(gt_c)]
        args += [y, modp, modr]
    in_specs.append(pl.BlockSpec((None, 1, d), lambda i: (l, 0, 0)))
    args.append(g.reshape(g.shape[0], 1, d))
    if mod is not None:
        sc_c, sh_c = mod
        in_specs += [bspec(sc_c), bspec(sh_c), rspec(sc_c), rspec(sh_c)]
        args += [modp, modp, modr, modr]
    out_specs, out_shape = [], []
    if out_x:
        out_specs.append(row_spec)
        out_shape.append(jax.ShapeDtypeStruct((rows, d), F32))
    out_specs.append(row_spec)
    out_shape.append(jax.ShapeDtypeStruct((rows, d), h_dtype))
    if out_slab:
        out_specs.append(pl.BlockSpec((tr * (d // LANES), LANES), lambda i: (i, 0)))
        out_shape.append(jax.ShapeDtypeStruct((rows * (d // LANES), LANES), F32))
    return pl.pallas_call(
        functools.partial(_rownorm_kernel, n_prompt_tiles=n_ptiles, has_resid=resid is not None,
                          has_mod=mod is not None, out_x=out_x, out_slab=out_slab, d=d),
        grid=(rows // tr,),
        in_specs=in_specs,
        out_specs=out_specs,
        out_shape=out_shape,
        compiler_params=_cparams(("arbitrary",), VMEM_LIMIT),
        name=name,
    )(*args)


def _l2norm_groups(y, width):
    parts = []
    for s in range(y.shape[1] // width):
        seg = y[:, s * width:(s + 1) * width]
        parts.append(seg * lax.rsqrt(jnp.sum(seg * seg, axis=-1, keepdims=True) + NORM_EPS))
    return parts


def _dn_conv_kernel(x_ref, w_ref, o_ref, xs_ref, *, tiles_per_seq, n_norm_tiles, dk):
    tr = x_ref.shape[0]
    j = pl.program_id(0)
    first = (pl.program_id(1) % tiles_per_seq) == 0

    @pl.when(first)
    def _():
        xs_ref[0:SUBLANES, :] = jnp.zeros((SUBLANES, xs_ref.shape[1]), F32)

    @pl.when(jnp.logical_not(first))
    def _():
        xs_ref[0:SUBLANES, :] = xs_ref[tr:tr + SUBLANES, :]

    xs_ref[SUBLANES:SUBLANES + tr, :] = x_ref[...]
    w = w_ref[...]
    nw = w.shape[0]
    acc = xs_ref[SUBLANES:SUBLANES + tr, :] * w[nw - 1:nw, :]
    for t in range(1, nw):
        acc = acc + xs_ref[SUBLANES - t:SUBLANES - t + tr, :] * w[nw - 1 - t:nw - t, :]
    y = _silu(acc)

    @pl.when(j < n_norm_tiles)
    def _():
        for s, seg in enumerate(_l2norm_groups(y, dk)):
            o_ref[:, s * dk:(s + 1) * dk] = seg

    @pl.when(j >= n_norm_tiles)
    def _():
        o_ref[...] = y


def _dn_conv_prompt(p, conv_w, l, *, n_rows, rows_per_seq, conv_ch, qk_dim, dk):
    tr = ROW_TILE
    tc = _pick_tile(qk_dim, 512, dk)
    return pl.pallas_call(
        functools.partial(_dn_conv_kernel, tiles_per_seq=rows_per_seq // tr, n_norm_tiles=2 * qk_dim // tc, dk=dk),
        grid=(conv_ch // tc, n_rows // tr),
        in_specs=[
            pl.BlockSpec((tr, tc), lambda j, i: (i, j)),
            pl.BlockSpec((None, conv_w.shape[1], tc), lambda j, i: (l, 0, j)),
        ],
        out_specs=pl.BlockSpec((tr, tc), lambda j, i: (i, j)),
        out_shape=jax.ShapeDtypeStruct((n_rows, conv_ch), F32),
        scratch_shapes=[pltpu.VMEM((tr + SUBLANES, tc), F32)],
        compiler_params=_cparams(("arbitrary", "arbitrary"), VMEM_LIMIT),
        name="dn_conv_prompt",
    )(p, conv_w)


def _dn_conv_step_kernel(x_ref, prev_ref, w_ref, o_ref, *, n_norm_tiles, dk):
    w = w_ref[...]
    nw = w.shape[0]
    acc = x_ref[...] * w[nw - 1:nw, :]
    for t in range(nw - 1):
        acc = acc + prev_ref[t] * w[t:t + 1, :]
    y = _silu(acc)

    @pl.when(pl.program_id(0) < n_norm_tiles)
    def _():
        for s, seg in enumerate(_l2norm_groups(y, dk)):
            o_ref[:, s * dk:(s + 1) * dk] = seg

    @pl.when(pl.program_id(0) >= n_norm_tiles)
    def _():
        o_ref[...] = y


def _dn_conv_step(p, prev, conv_w, l, *, row0, rows, conv_ch, qk_dim, dk):
    tc = _pick_tile(qk_dim, 512, dk)
    rb = row0 // rows
    return pl.pallas_call(
        functools.partial(_dn_conv_step_kernel, n_norm_tiles=2 * qk_dim // tc, dk=dk),
        grid=(conv_ch // tc,),
        in_specs=[
            pl.BlockSpec((rows, tc), lambda j: (rb, j)),
            pl.BlockSpec((prev.shape[0], rows, tc), lambda j: (0, 0, j)),
            pl.BlockSpec((None, conv_w.shape[1], tc), lambda j: (l, 0, j)),
        ],
        out_specs=pl.BlockSpec((rows, tc), lambda j: (0, j)),
        out_shape=jax.ShapeDtypeStruct((rows, conv_ch), F32),
        compiler_params=_cparams(("arbitrary",), VMEM_LIMIT),
        name="dn_conv_step",
    )(p, prev, conv_w)


def _dn_gate_kernel(ba_ref, alog_ref, dtb_ref, beta_ref, g_ref, gc_ref, *, hv, chunk):
    ba = ba_ref[...]
    beta_ref[...] = jax.nn.sigmoid(ba[:, :hv])
    x = ba[:, hv:] + dtb_ref[...]
    g = -jnp.exp(alog_ref[...]) * (jnp.maximum(x, 0.0) + jnp.log1p(jnp.exp(-jnp.abs(x))))
    g_ref[...] = g
    tril = (lax.broadcasted_iota(I32, (chunk, chunk), 0) >= lax.broadcasted_iota(I32, (chunk, chunk), 1)).astype(BF16)
    for c in range(ba.shape[0] // chunk):
        h1, h2, h3 = _split3(g[c * chunk:(c + 1) * chunk, :])
        gc = (jnp.dot(tril, h1, preferred_element_type=F32) + jnp.dot(tril, h2, preferred_element_type=F32)
              + jnp.dot(tril, h3, preferred_element_type=F32))
        gc_ref[c * chunk:(c + 1) * chunk, :] = gc


def _dn_gates(ba, a_log, dt_bias, l, *, hv):
    rows = ba.shape[0]
    tr = ROW_TILE
    spec = pl.BlockSpec((tr, hv), lambda i: (i, 0))
    pspec = pl.BlockSpec((None, 1, hv), lambda i: (l, 0, 0))
    shp = jax.ShapeDtypeStruct((rows, hv), F32)
    return pl.pallas_call(
        functools.partial(_dn_gate_kernel, hv=hv, chunk=DN_CHUNK),
        grid=(rows // tr,),
        in_specs=[pl.BlockSpec((tr, 2 * hv), lambda i: (i, 0)), pspec, pspec],
        out_specs=[spec, spec, spec],
        out_shape=[shp, shp, shp],
        compiler_params=_cparams(("arbitrary",)),
        name="dn_gates",
    )(ba, a_log.reshape(a_log.shape[0], 1, hv), dt_bias.reshape(dt_bias.shape[0], 1, hv))


def _unit_lower_inverse(lmat, eye_f, blk):
    c = lmat.shape[0]
    ld = jnp.where(blk, lmat, 0.0)
    lo = lmat - ld
    m = -ld
    dinv = eye_f + m
    p = m
    span = 2
    while span < DN_SOLVE_BLOCK:
        p = _bdot(p, p)
        dinv = dinv + _bdot(dinv, p)
        span *= 2
    n = -_bdot(dinv, lo)
    x = eye_f + n
    p = n
    span = 2
    while span < c // DN_SOLVE_BLOCK:
        p = _bdot(p, p)
        x = x + _bdot(x, p)
        span *= 2
    return _bdot(x, dinv)


def _gated_rmsnorm(o, z, g):
    y = o * lax.rsqrt(jnp.mean(o * o, axis=-1, keepdims=True) + NORM_EPS) * g
    return y * _silu(z)


def _dn_chunk_kernel(q_ref, k_ref, v_ref, z_ref, beta_ref, gc_ref, gl_ref, ng_ref, og_ref, sout_ref, s_ref,
                     *, heads, rep, dk, dv):
    c = q_ref.shape[0]
    ci = pl.program_id(2)

    @pl.when(ci == 0)
    def _():
        s_ref[...] = jnp.zeros(s_ref.shape, F32)

    row = lax.broadcasted_iota(I32, (c, c), 0)
    col = lax.broadcasted_iota(I32, (c, c), 1)
    eye = row == col
    eye_f = eye.astype(F32)
    causal = row >= col
    strict = row > col
    blk = (row // DN_SOLVE_BLOCK) == (col // DN_SOLVE_BLOCK)
    ng = ng_ref[...]
    beta = beta_ref[...]
    gc = gc_ref[...]
    gl = gl_ref[...]
    for s in range(heads):
        hq = s // rep
        qh = q_ref[:, hq * dk:(hq + 1) * dk] * (dk ** -0.5)
        kh = k_ref[:, hq * dk:(hq + 1) * dk]
        vh = v_ref[:, s * dv:(s + 1) * dv]
        bcol = beta[:, s:s + 1]
        gcol = gc[:, s:s + 1]
        grow = jnp.sum(jnp.where(eye, gcol, 0.0), axis=0, keepdims=True)
        decay = jnp.exp(jnp.where(causal, gcol - grow, -jnp.inf))
        egc = jnp.exp(gcol)
        kb = kh * bcol
        kq = _bdot_nt(jnp.concatenate([kb, qh], axis=0), kh)
        lmat = jnp.where(strict, kq[:c] * decay, 0.0)
        attn = kq[c:] * decay
        tinv = _unit_lower_inverse(lmat, eye_f, blk)
        sol = _bdot(tinv, jnp.concatenate([vh * bcol, kb * egc], axis=1))
        u = sol[:, :dv]
        w = sol[:, dv:]
        st = s_ref[s]
        wq = _bdot(jnp.concatenate([w, qh * egc], axis=0), st)
        v_new = u - wq[:c]
        o = wq[c:] + _bdot(attn, v_new)
        glast = gl[:, s:s + 1]
        kd = kh * jnp.exp(glast - gcol)
        s_ref[s] = st * jnp.exp(glast) + _bdot_tn(kd, v_new)
        og_ref[:, s * dv:(s + 1) * dv] = _gated_rmsnorm(o, z_ref[:, s * dv:(s + 1) * dv], ng).astype(og_ref.dtype)

    @pl.when(ci == pl.num_programs(2) - 1)
    def _():
        sout_ref[...] = s_ref[...]


def _dn_chunks(qkv, p, beta3, gc3, gl4, norm_g, l, *, batch, seq, hk, hv, dk, dv, conv_ch):
    c = DN_CHUNK
    hb = min(DN_HEADS_PER_STEP, hv)
    rep = hv // hk
    nc = seq // c
    qw = (hb // rep) * dk
    vw = hb * dv
    qk_dim = hk * dk

    def rowmap(off):
        return lambda b, g, i: (b * nc + i, off + g)

    return pl.pallas_call(
        functools.partial(_dn_chunk_kernel, heads=hb, rep=rep, dk=dk, dv=dv),
        grid=(batch, hv // hb, nc),
        in_specs=[
            pl.BlockSpec((c, qw), rowmap(0)),
            pl.BlockSpec((c, qw), rowmap(qk_dim // qw)),
            pl.BlockSpec((c, vw), rowmap(2 * qk_dim // vw)),
            pl.BlockSpec((c, vw), rowmap(conv_ch // vw)),
            pl.BlockSpec((None, c, hb), lambda b, g, i: (g, b * nc + i, 0)),
            pl.BlockSpec((None, c, hb), lambda b, g, i: (g, b * nc + i, 0)),
            pl.BlockSpec((None, None, 1, hb), lambda b, g, i: (g, b * nc + i, 0, 0)),
            pl.BlockSpec((None, 1, dv), lambda b, g, i: (l, 0, 0)),
        ],
        out_specs=[
            pl.BlockSpec((c, vw), lambda b, g, i: (b * nc + i, g)),
            pl.BlockSpec((None, hb, dk, dv), lambda b, g, i: (b, g, 0, 0)),
        ],
        out_shape=[
            jax.ShapeDtypeStruct((batch * seq, hv * dv), BF16),
            jax.ShapeDtypeStruct((batch, hv, dk, dv), F32),
        ],
        scratch_shapes=[pltpu.VMEM((hb, dk, dv), F32)],
        compiler_params=_cparams(("arbitrary", "arbitrary", "arbitrary"), VMEM_LIMIT),
        name="dn_chunks",
    )(qkv, qkv, qkv, p, beta3, gc3, gl4, norm_g.reshape(norm_g.shape[0], 1, dv))


def _dn_step_kernel(q_ref, k_ref, v_ref, z_ref, beta_ref, g_ref, ng_ref, sin_ref, og_ref, sout_ref,
                    *, heads, rep, dk, dv):
    eye = lax.broadcasted_iota(I32, (dk, dk), 0) == lax.broadcasted_iota(I32, (dk, dk), 1)
    ng = ng_ref[...]
    beta = beta_ref[...]
    g = g_ref[...]

    def column(r):
        return jnp.sum(jnp.where(eye, r, 0.0), axis=1, keepdims=True)

    for s in range(heads):
        hq = s // rep
        q_col = column(q_ref[:, hq * dk:(hq + 1) * dk])
        k_col = column(k_ref[:, hq * dk:(hq + 1) * dk])
        v_row = v_ref[:, s * dv:(s + 1) * dv]
        a = jnp.exp(g[:, s:s + 1])
        st = sin_ref[s]
        ks = jnp.sum(st * k_col, axis=0, keepdims=True)
        v_new = beta[:, s:s + 1] * (v_row - a * ks)
        s_new = a * st + k_col * v_new
        sout_ref[s] = s_new
        o = jnp.sum(s_new * q_col, axis=0, keepdims=True) * (dk ** -0.5)
        og_ref[:, s * dv:(s + 1) * dv] = _gated_rmsnorm(o, z_ref[:, s * dv:(s + 1) * dv], ng)


def _dn_step(qkv3, z3, beta4, g4, norm_g, l, s_in, *, hk, hv, dk, dv):
    bs = s_in.shape[0]
    hb = min(DN_HEADS_PER_STEP, hv)
    rep = hv // hk
    qw = (hb // rep) * dk
    vw = hb * dv
    qk_dim = hk * dk
    return pl.pallas_call(
        functools.partial(_dn_step_kernel, heads=hb, rep=rep, dk=dk, dv=dv),
        grid=(bs, hv // hb),
        in_specs=[
            pl.BlockSpec((None, 1, qw), lambda b, g: (b, 0, g)),
            pl.BlockSpec((None, 1, qw), lambda b, g: (b, 0, qk_dim // qw + g)),
            pl.BlockSpec((None, 1, vw), lambda b, g: (b, 0, 2 * qk_dim // vw + g)),
            pl.BlockSpec((None, 1, vw), lambda b, g: (b, 0, g)),
            pl.BlockSpec((None, None, 1, hb), lambda b, g: (g, b, 0, 0)),
            pl.BlockSpec((None, None, 1, hb), lambda b, g: (g, b, 0, 0)),
            pl.BlockSpec((None, 1, dv), lambda b, g: (l, 0, 0)),
            pl.BlockSpec((None, hb, dk, dv), lambda b, g: (b, g, 0, 0)),
        ],
        out_specs=[
            pl.BlockSpec((None, 1, vw), lambda b, g: (b, 0, g)),
            pl.BlockSpec((None, hb, dk, dv), lambda b, g: (b, g, 0, 0)),
        ],
        out_shape=[
            jax.ShapeDtypeStruct((bs, 1, hv * dv), F32),
            jax.ShapeDtypeStruct(s_in.shape, F32),
        ],
        compiler_params=_cparams(("arbitrary", "arbitrary"), VMEM_LIMIT),
        name="dn_step",
    )(qkv3, qkv3, qkv3, z3, beta4, g4, norm_g.reshape(norm_g.shape[0], 1, dv), s_in)


def _head_groups(x, hb):
    rows, h = x.shape
    return x.reshape(rows, h // hb, hb).transpose(1, 0, 2)


def _deltanet(hm, j, w, st_conv, st_rec, dims):
    n_p, seq, batch, bs, np_rows = dims["n_p"], dims["seq"], dims["batch"], dims["bs"], dims["np_rows"]
    dk, dv = st_rec.shape[3], st_rec.shape[4]
    hv = st_rec.shape[2]
    conv_ch = st_conv.shape[3]
    v_dim = hv * dv
    qk_dim = (conv_ch - v_dim) // 2
    hk = qk_dim // dk
    hb = min(DN_HEADS_PER_STEP, hv)

    p = _mm(hm, w["dn_w_in"], j, name="dn_in_proj")
    ba = p[:, conv_ch + v_dim:]
    beta, g, gc = _dn_gates(ba, w["dn_a_log"], w["dn_dt_bias"], j, hv=hv)

    qkv = _dn_conv_prompt(p, w["dn_conv_w"], j, n_rows=n_p, rows_per_seq=seq, conv_ch=conv_ch, qk_dim=qk_dim, dk=dk)
    gl = gc[DN_CHUNK - 1:n_p:DN_CHUNK]
    og_p, rec_p = _dn_chunks(
        qkv, p, _head_groups(beta[:n_p], hb), _head_groups(gc[:n_p], hb),
        _head_groups(gl, hb)[:, :, None, :], w["dn_norm_g"], j,
        batch=batch, seq=seq, hk=hk, hv=hv, dk=dk, dv=dv, conv_ch=conv_ch)
    nw = w["dn_conv_w"].shape[1]
    conv_p = p[:n_p, :conv_ch].reshape(batch, seq, conv_ch)[:, seq - (nw - 1):]

    prev = jnp.transpose(st_conv[j], (1, 0, 2))
    qkv_s = _dn_conv_step(p, prev, w["dn_conv_w"], j, row0=n_p, rows=bs, conv_ch=conv_ch, qk_dim=qk_dim, dk=dk)
    z_s = p[n_p:n_p + bs, conv_ch:conv_ch + v_dim]
    og_s, rec_s = _dn_step(
        qkv_s[:, None, :], z_s[:, None, :], _head_groups(beta[n_p:n_p + bs], hb)[:, :, None, :],
        _head_groups(g[n_p:n_p + bs], hb)[:, :, None, :], w["dn_norm_g"], j, st_rec[j],
        hk=hk, hv=hv, dk=dk, dv=dv)
    conv_s = jnp.concatenate([st_conv[j][:, 1:], p[n_p:n_p + bs, None, :conv_ch]], axis=1)

    og = jnp.concatenate(
        [og_p, og_s.reshape(bs, v_dim).astype(BF16), jnp.zeros((np_rows - n_p - bs, v_dim), BF16)], axis=0)
    mix = _mm(og, w["dn_w_out"], j, name="dn_out_proj")
    return mix, conv_p, rec_p, conv_s, rec_s


def _rope_kernel(x_ref, cos_ref, sin_ref, o_ref, *, hd):
    cos = cos_ref[...]
    sin = sin_ref[...]
    lane = lax.broadcasted_iota(I32, cos.shape, 1)
    first_half = (lane % hd) < hd // 2
    for s in range(x_ref.shape[1] // LANES):
        x = x_ref[:, s * LANES:(s + 1) * LANES]
        rot = jnp.where(first_half, pltpu.roll(x, LANES - hd // 2, axis=1), pltpu.roll(x, hd // 2, axis=1))
        o_ref[:, s * LANES:(s + 1) * LANES] = x * cos + rot * sin


def _rope(p2, cos, sin, *, width, hd):
    rows = p2.shape[0]
    tr = ROW_TILE
    tc = _pick_tile(width, 512, LANES)
    return pl.pallas_call(
        functools.partial(_rope_kernel, hd=hd),
        grid=(rows // tr, width // tc),
        in_specs=[
            pl.BlockSpec((tr, tc), lambda i, j: (i, j)),
            pl.BlockSpec((tr, LANES), lambda i, j: (i, 0)),
            pl.BlockSpec((tr, LANES), lambda i, j: (i, 0)),
        ],
        out_specs=pl.BlockSpec((tr, tc), lambda i, j: (i, j)),
        out_shape=jax.ShapeDtypeStruct((rows, width), F32),
        compiler_params=_cparams(("arbitrary", "arbitrary")),
        name="swa_rope",
    )(p2, cos, sin)


def _swa_prompt_kernel(q_ref, kp_ref, ko_ref, vp_ref, vo_ref, sink_ref, o_ref, *, kv, group, hd, window):
    blk = q_ref.shape[0]
    i = pl.program_id(1)
    r = lax.broadcasted_iota(I32, (blk, 2 * blk), 0)
    c = lax.broadcasted_iota(I32, (blk, 2 * blk), 1)
    mask = (c - blk <= r) & (c - blk > r - window) & ((i > 0) | (c >= blk))
    sinks = sink_ref[...]
    scale = hd ** -0.5
    for n in range(kv):
        kk = jnp.concatenate([kp_ref[:, n * hd:(n + 1) * hd], ko_ref[:, n * hd:(n + 1) * hd]], axis=0).astype(BF16)
        vv = jnp.concatenate([vp_ref[:, n * hd:(n + 1) * hd], vo_ref[:, n * hd:(n + 1) * hd]], axis=0).astype(BF16)
        for gq in range(group):
            h = n * group + gq
            qh = q_ref[:, h * hd:(h + 1) * hd]
            s = _bdot_nt(qh, kk) * scale
            s = jnp.where(mask, s, -jnp.inf)
            sink = sinks[:, h:h + 1]
            m = jnp.maximum(jnp.max(s, axis=-1, keepdims=True), sink)
            e = jnp.exp(s - m)
            pn = e / (jnp.sum(e, axis=-1, keepdims=True) + jnp.exp(sink - m))
            o_ref[:, h * hd:(h + 1) * hd] = jnp.dot(pn.astype(BF16), vv, preferred_element_type=F32).astype(o_ref.dtype)


def _swa_prompt(rot, p2, sinks, l, *, batch, seq, heads, kv, hd):
    blk = SWA_WINDOW
    nb = seq // blk
    qw = heads * hd
    kw = kv * hd
    kcol = qw // kw

    def own(off):
        return lambda b, i: (b * nb + i, off)

    def prev(off):
        return lambda b, i: (b * nb + jnp.maximum(i - 1, 0), off)

    return pl.pallas_call(
        functools.partial(_swa_prompt_kernel, kv=kv, group=heads // kv, hd=hd, window=SWA_WINDOW),
        grid=(batch, nb),
        in_specs=[
            pl.BlockSpec((blk, qw), own(0)),
            pl.BlockSpec((blk, kw), prev(kcol)),
            pl.BlockSpec((blk, kw), own(kcol)),
            pl.BlockSpec((blk, kw), prev(kcol + 1)),
            pl.BlockSpec((blk, kw), own(kcol + 1)),
            pl.BlockSpec((None, 1, heads), lambda b, i: (l, 0, 0)),
        ],
        out_specs=pl.BlockSpec((blk, qw), own(0)),
        out_shape=jax.ShapeDtypeStruct((batch * seq, qw), BF16),
        compiler_params=_cparams(("arbitrary", "arbitrary"), VMEM_LIMIT),
        name="swa_prompt_attn",
    )(rot, rot, rot, p2, p2, sinks.reshape(sinks.shape[0], 1, heads))


def _swa_step_kernel(q_ref, kn_ref, vn_ref, ck_ref, cv_ref, sink_ref, o_ref, *, kv, group, hd, first_valid):
    buf = ck_ref.shape[0]
    valid = lax.broadcasted_iota(I32, (group, buf), 1) >= first_valid
    scale = hd ** -0.5
    for n in range(kv):
        qg = q_ref[n * group:(n + 1) * group, :]
        kc = ck_ref[:, n * hd:(n + 1) * hd]
        vc = cv_ref[:, n * hd:(n + 1) * hd]
        kn = kn_ref[n:n + 1, :]
        vn = vn_ref[n:n + 1, :]
        sink = sink_ref[n * group:(n + 1) * group, :]
        s = jnp.where(valid, _bdot_nt(qg, kc) * scale, -jnp.inf)
        s_new = jnp.sum(qg.astype(BF16).astype(F32) * kn.astype(BF16).astype(F32), axis=-1, keepdims=True) * scale
        m = jnp.maximum(jnp.maximum(jnp.max(s, axis=-1, keepdims=True), s_new), sink)
        e = jnp.exp(s - m)
        e_new = jnp.exp(s_new - m)
        den = jnp.sum(e, axis=-1, keepdims=True) + e_new + jnp.exp(sink - m)
        o = _bdot(e / den, vc) + (e_new / den) * vn
        o_ref[n * group:(n + 1) * group, :] = o


def _swa_step(q3, kn3, vn3, ck, cv, sinks, l, *, heads, kv, hd):
    bs, buf = ck.shape[0], ck.shape[1]
    kw = kv * hd
    first_valid = max(buf - SWA_WINDOW + 1, 0)
    return pl.pallas_call(
        functools.partial(_swa_step_kernel, kv=kv, group=heads // kv, hd=hd, first_valid=first_valid),
        grid=(bs,),
        in_specs=[
            pl.BlockSpec((None, heads, hd), lambda b: (b, 0, 0)),
            pl.BlockSpec((None, kv, hd), lambda b: (b, 0, 0)),
            pl.BlockSpec((None, kv, hd), lambda b: (b, 0, 0)),
            pl.BlockSpec((None, buf, kw), lambda b: (b, 0, 0)),
            pl.BlockSpec((None, buf, kw), lambda b: (b, 0, 0)),
            pl.BlockSpec((None, heads, 1), lambda b: (l, 0, 0)),
        ],
        out_specs=pl.BlockSpec((None, heads, hd), lambda b: (b, 0, 0)),
        out_shape=jax.ShapeDtypeStruct((bs, heads, hd), F32),
        compiler_params=_cparams(("arbitrary",)),
        name="swa_step_attn",
    )(q3, kn3, vn3, ck, cv, sinks.reshape(sinks.shape[0], heads, 1))


def _rope_tables(pos, hd):
    half = hd // 2
    inv = ROPE_THETA ** (-jnp.arange(half, dtype=F32) / half)
    ang = pos.astype(F32)[:, None] * inv[None, :]
    cos, sin = jnp.cos(ang), jnp.sin(ang)
    reps = LANES // hd
    return jnp.tile(jnp.concatenate([cos, cos], axis=1), (1, reps)), jnp.tile(jnp.concatenate([-sin, sin], axis=1), (1, reps))


def _swa(hm, j, w, cache_k, cache_v, dims, past_len):
    n_p, seq, batch, bs, np_rows = dims["n_p"], dims["seq"], dims["batch"], dims["bs"], dims["np_rows"]
    kv, hd = cache_k.shape[3], cache_k.shape[4]
    buf = cache_k.shape[2]
    heads = w["swa_sinks"].shape[1]
    qw, kw = heads * hd, kv * hd

    p2 = _mm(hm, w["swa_w_in"], j, w["swa_b_in"], name="swa_in_proj")
    pos = jnp.concatenate([jnp.tile(jnp.arange(seq), batch), jnp.full((np_rows - n_p,), past_len)])
    cos, sin = _rope_tables(pos, hd)
    rot = _rope(p2, cos, sin, width=qw + kw, hd=hd)

    o_p = _swa_prompt(rot, p2, w["swa_sinks"], j, batch=batch, seq=seq, heads=heads, kv=kv, hd=hd)
    keep = min(SWA_WINDOW, seq)
    k_p = rot[:n_p, qw:].reshape(batch, seq, kv, hd)[:, seq - keep:]
    v_p = p2[:n_p, qw + kw:].reshape(batch, seq, kv, hd)[:, seq - keep:]

    q_s = rot[n_p:n_p + bs, :qw].reshape(bs, heads, hd)
    kn = rot[n_p:n_p + bs, qw:].reshape(bs, kv, hd)
    vn = p2[n_p:n_p + bs, qw + kw:].reshape(bs, kv, hd)
    o_s = _swa_step(q_s, kn, vn, cache_k[j].reshape(bs, buf, kw), cache_v[j].reshape(bs, buf, kw),
                    w["swa_sinks"], j, heads=heads, kv=kv, hd=hd)
    k_s = jnp.concatenate([cache_k[j], kn[:, None]], axis=1)[:, 1:]
    v_s = jnp.concatenate([cache_v[j], vn[:, None]], axis=1)[:, 1:]

    o = jnp.concatenate([o_p, o_s.reshape(bs, qw).astype(BF16), jnp.zeros((np_rows - n_p - bs, qw), BF16)], axis=0)
    mix = _mm(o, w["swa_w_out"], j, w["swa_b_out"], name="swa_out_proj")
    return mix, k_p, v_p, k_s, v_s


def _router_kernel(h_ref, w_ref, b_ref, gate_ref, chosen_ref, *, n_valid_rows, groups, topk_groups, top_k):
    tr, e = gate_ref.shape
    h = h_ref[...]
    w = w_ref[...]
    wh = w.astype(BF16)
    wl = (w - wh.astype(F32)).astype(BF16)
    logits = jnp.dot(h, wh, preferred_element_type=F32) + jnp.dot(h, wl, preferred_element_type=F32)
    scores = jax.nn.sigmoid(logits)
    sel = scores + b_ref[...]
    lane = lax.broadcasted_iota(I32, (tr, e), 1)
    grp = lane // (e // groups)
    neg = -jnp.inf

    def first_max(v, ids, sentinel):
        m = jnp.max(v, axis=-1, keepdims=True)
        return m, jnp.min(jnp.where(v == m, ids, sentinel), axis=-1, keepdims=True)

    gscore = jnp.zeros((tr, e), F32)
    for gi in range(groups):
        v = jnp.where(grp == gi, sel, neg)
        m1, i1 = first_max(v, lane, e)
        m2 = jnp.max(jnp.where(lane == i1, neg, v), axis=-1, keepdims=True)
        gscore = jnp.where(grp == gi, m1 + m2, gscore)
    gmask = jnp.zeros((tr, e), jnp.bool_)
    for _ in range(topk_groups):
        _, gi = first_max(gscore, grp, groups)
        hit = grp == gi
        gmask = gmask | hit
        gscore = jnp.where(hit, neg, gscore)
    cur = jnp.where(gmask, sel, neg)
    chosen = jnp.zeros((tr, e), jnp.bool_)
    for _ in range(top_k):
        _, ei = first_max(cur, lane, e)
        hit = lane == ei
        chosen = chosen | hit
        cur = jnp.where(hit, neg, cur)
    wts = jnp.where(chosen, scores, 0.0)
    gates = wts / (jnp.sum(wts, axis=-1, keepdims=True) + 1e-20) * ROUTED_SCALE
    real = (pl.program_id(0) * tr + lax.broadcasted_iota(I32, (tr, e), 0)) < n_valid_rows
    gate_ref[...] = jnp.where(real, gates, 0.0)
    chosen_ref[...] = (chosen & real).astype(I32)


def _router(hf, router_w, router_b, l, *, n_valid_rows):
    rows, d = hf.shape
    e = router_w.shape[2]
    tr = ROW_TILE
    return pl.pallas_call(
        functools.partial(_router_kernel, n_valid_rows=n_valid_rows, groups=MOE_GROUPS,
                          topk_groups=MOE_TOPK_GROUPS, top_k=MOE_TOP_K),
        grid=(rows // tr,),
        in_specs=[
            pl.BlockSpec((tr, d), lambda i: (i, 0)),
            pl.BlockSpec((None, d, e), lambda i: (l, 0, 0)),
            pl.BlockSpec((None, 1, e), lambda i: (l, 0, 0)),
        ],
        out_specs=[pl.BlockSpec((tr, e), lambda i: (i, 0)), pl.BlockSpec((tr, e), lambda i: (i, 0))],
        out_shape=[jax.ShapeDtypeStruct((rows, e), F32), jax.ShapeDtypeStruct((rows, e), I32)],
        compiler_params=_cparams(("arbitrary",)),
        name="moe_router",
    )(hf, router_w, router_b.reshape(router_b.shape[0], 1, e))


def _row_copy(src_hbm, dst_vmem, sem, token, slot, slab):
    return pltpu.make_async_copy(src_hbm.at[token], dst_vmem.at[pl.ds(pl.multiple_of(slot * slab, slab), slab), :], sem)


def _moe_gather_kernel(tok_ref, nvalid_ref, x_ref, o_ref, buf_ref, sem, *, tm, slab):
    i = pl.program_id(0)

    @pl.when(i < nvalid_ref[0])
    def _():
        def issue(r, carry):
            _row_copy(x_ref, buf_ref, sem, tok_ref[i * tm + r], r, slab).start()
            return carry

        lax.fori_loop(0, tm, issue, 0)

        def drain(r, carry):
            _row_copy(x_ref, buf_ref, sem, 0, r, slab).wait()
            return carry

        lax.fori_loop(0, tm, drain, 0)
        for s in range(slab):
            o_ref[:, s * LANES:(s + 1) * LANES] = buf_ref[pl.ds(s, tm, stride=slab), :].astype(o_ref.dtype)

    @pl.when(i >= nvalid_ref[0])
    def _():
        o_ref[...] = jnp.zeros(o_ref.shape, o_ref.dtype)


def _moe_gather(row_token, n_valid_tiles, x_slab, *, n_tiles, d):
    tm = MOE_TILE
    slab = d // LANES
    gs = pltpu.PrefetchScalarGridSpec(
        num_scalar_prefetch=2,
        grid=(n_tiles,),
        in_specs=[pl.BlockSpec(memory_space=pl.ANY)],
        out_specs=pl.BlockSpec((tm, d), lambda i, tok, nv: (i, 0)),
        scratch_shapes=[pltpu.VMEM((tm * slab, LANES), F32), pltpu.SemaphoreType.DMA(())],
    )
    return pl.pallas_call(
        functools.partial(_moe_gather_kernel, tm=tm, slab=slab),
        grid_spec=gs,
        out_shape=jax.ShapeDtypeStruct((n_tiles * tm, d), BF16),
        compiler_params=_cparams(("arbitrary",), VMEM_LIMIT),
        name="moe_gather",
    )(row_token, n_valid_tiles, x_slab)


def _expert_changed(te_ref, i):
    return (i == 0) | (te_ref[i] != te_ref[jnp.maximum(i - 1, 0)])


def _moe_up_kernel(te_ref, nvalid_ref, x_ref, gate_ref, wg_ref, wu_ref, o_ref, wgc_ref, wuc_ref):
    i = pl.program_id(1)

    @pl.when(i < nvalid_ref[0])
    def _():
        @pl.when(_expert_changed(te_ref, i))
        def _():
            wgc_ref[...] = wg_ref[...].astype(BF16)
            wuc_ref[...] = wu_ref[...].astype(BF16)

        x = x_ref[...]
        g = jnp.dot(x, wgc_ref[...], preferred_element_type=F32)
        u = jnp.dot(x, wuc_ref[...], preferred_element_type=F32)
        o_ref[...] = (_silu(g) * u * gate_ref[...]).astype(o_ref.dtype)

    @pl.when(i >= nvalid_ref[0])
    def _():
        o_ref[...] = jnp.zeros(o_ref.shape, o_ref.dtype)


def _moe_up(tile_expert, n_valid_tiles, xs, row_gate, w_in, l, *, n_tiles):
    tm = MOE_TILE
    d = xs.shape[1]
    de = w_in.shape[3] // 2
    tj = _pick_tile(de, 512, LANES)
    nj = de // tj

    def tile(i, nv):
        return jnp.minimum(i, nv[0] - 1)

    gs = pltpu.PrefetchScalarGridSpec(
        num_scalar_prefetch=2,
        grid=(nj, n_tiles),
        in_specs=[
            pl.BlockSpec((tm, d), lambda j, i, te, nv: (tile(i, nv), 0)),
            pl.BlockSpec((tm, 1), lambda j, i, te, nv: (tile(i, nv), 0)),
            pl.BlockSpec((None, None, d, tj), lambda j, i, te, nv: (l, te[i], 0, j)),
            pl.BlockSpec((None, None, d, tj), lambda j, i, te, nv: (l, te[i], 0, nj + j)),
        ],
        out_specs=pl.BlockSpec((tm, tj), lambda j, i, te, nv: (i, j)),
        scratch_shapes=[pltpu.VMEM((d, tj), BF16), pltpu.VMEM((d, tj), BF16)],
    )
    return pl.pallas_call(
        _moe_up_kernel,
        grid_spec=gs,
        out_shape=jax.ShapeDtypeStruct((n_tiles * tm, de), BF16),
        compiler_params=_cparams(("arbitrary", "arbitrary"), VMEM_LIMIT),
        name="moe_up",
    )(tile_expert, n_valid_tiles, xs, row_gate, w_in, w_in)


def _moe_down_kernel(te_ref, nvalid_ref, h_ref, w_ref, o_ref, wc_ref, *, tm):
    i = pl.program_id(1)

    @pl.when(i < nvalid_ref[0])
    def _():
        @pl.when(_expert_changed(te_ref, i))
        def _():
            wc_ref[...] = w_ref[...].astype(BF16)

        y = jnp.dot(h_ref[...], wc_ref[...], preferred_element_type=F32)
        nslab = y.shape[1] // LANES
        for s in range(nslab):
            o_ref[pl.ds(s, tm, stride=nslab), :] = y[:, s * LANES:(s + 1) * LANES]

    @pl.when(i >= nvalid_ref[0])
    def _():
        o_ref[...] = jnp.zeros(o_ref.shape, o_ref.dtype)


def _moe_down(tile_expert, n_valid_tiles, h, w_out, l, *, n_tiles):
    tm = MOE_TILE
    de, d = w_out.shape[2], w_out.shape[3]
    tn = _pick_tile(d, 2048, LANES)
    nn = d // tn

    def tile(i, nv):
        return jnp.minimum(i, nv[0] - 1)

    gs = pltpu.PrefetchScalarGridSpec(
        num_scalar_prefetch=2,
        grid=(nn, n_tiles),
        in_specs=[
            pl.BlockSpec((tm, de), lambda j, i, te, nv: (tile(i, nv), 0)),
            pl.BlockSpec((None, None, de, tn), lambda j, i, te, nv: (l, te[i], 0, j)),
        ],
        out_specs=pl.BlockSpec((None, tm * (tn // LANES), LANES), lambda j, i, te, nv: (j, i, 0)),
        scratch_shapes=[pltpu.VMEM((de, tn), BF16)],
    )
    return pl.pallas_call(
        functools.partial(_moe_down_kernel, tm=tm),
        grid_spec=gs,
        out_shape=jax.ShapeDtypeStruct((nn, n_tiles * tm * (tn // LANES), LANES), F32),
        compiler_params=_cparams(("arbitrary", "arbitrary"), VMEM_LIMIT),
        name="moe_down",
    )(tile_expert, n_valid_tiles, h, w_out)


def _moe_combine_kernel(dest_ref, ys_ref, sh_ref, o_ref, buf_ref, sem, *, tr, k, slab, nn):
    i = pl.program_id(0)

    def issue(r, carry):
        for c in range(nn):
            _row_copy(ys_ref.at[c], buf_ref.at[c], sem, dest_ref[i * tr * k + r], r, slab).start()
        return carry

    lax.fori_loop(0, tr * k, issue, 0)

    def drain(r, carry):
        for c in range(nn):
            _row_copy(ys_ref.at[c], buf_ref.at[c], sem, 0, r, slab).wait()
        return carry

    lax.fori_loop(0, tr * k, drain, 0)
    for c in range(nn):
        for s in range(slab):
            acc = sh_ref[:, (c * slab + s) * LANES:(c * slab + s + 1) * LANES]
            for e in range(k):
                acc = acc + buf_ref[c, pl.ds(e * slab + s, tr, stride=k * slab), :]
            o_ref[:, (c * slab + s) * LANES:(c * slab + s + 1) * LANES] = acc


def _moe_combine(dest, ys, shared, *, k):
    rows, d = shared.shape
    nn, _, _ = ys.shape
    slab = d // LANES // nn
    tr = 64
    ys4 = ys.reshape(nn, -1, slab, LANES)
    gs = pltpu.PrefetchScalarGridSpec(
        num_scalar_prefetch=1,
        grid=(rows // tr,),
        in_specs=[pl.BlockSpec(memory_space=pl.ANY), pl.BlockSpec((tr, d), lambda i, dst: (i, 0))],
        out_specs=pl.BlockSpec((tr, d), lambda i, dst: (i, 0)),
        scratch_shapes=[pltpu.VMEM((nn, tr * k * slab, LANES), F32), pltpu.SemaphoreType.DMA(())],
    )
    return pl.pallas_call(
        functools.partial(_moe_combine_kernel, tr=tr, k=k, slab=slab, nn=nn),
        grid_spec=gs,
        out_shape=jax.ShapeDtypeStruct((rows, d), F32),
        compiler_params=_cparams(("arbitrary",), VMEM_LIMIT),
        name="moe_combine",
    )(dest, ys4, shared)


def _dispatch_plan(gates, chosen, *, k, tm):
    rows, e = gates.shape
    n_tiles = (rows * k) // tm + e
    r_max = n_tiles * tm
    sel = chosen > 0
    cnt = jnp.sum(sel, axis=0, dtype=I32)
    rank = jnp.cumsum(sel.astype(I32), axis=0) - 1
    tiles_e = (cnt + tm - 1) // tm
    tile_end = jnp.cumsum(tiles_e)
    row_off = (tile_end - tiles_e) * tm
    n_valid = tile_end[-1]
    tile_ids = jnp.arange(n_tiles, dtype=I32)
    tile_expert = jnp.minimum(jnp.searchsorted(tile_end, tile_ids, side="right"), e - 1).astype(I32)
    last_valid_expert = tile_expert[jnp.maximum(n_valid - 1, 0)]
    tile_expert = jnp.where(tile_ids < n_valid, tile_expert, last_valid_expert)

    eid = lax.broadcasted_iota(I32, (rows, e), 1)
    slot = jnp.cumsum(sel.astype(I32), axis=1) - 1
    onehot = sel[:, :, None] & (slot[:, :, None] == jnp.arange(k, dtype=I32)[None, None, :])
    has = jnp.any(onehot, axis=1)
    top_e = jnp.where(has, jnp.sum(jnp.where(onehot, eid[:, :, None], 0), axis=1), e)
    dest_dense = row_off[None, :] + rank
    dest = jnp.sum(jnp.where(onehot, dest_dense[:, :, None], 0), axis=1)

    pair_e = top_e.reshape(-1)
    order = jnp.argsort(pair_e, stable=True).astype(I32)
    sorted_tok = order // k
    sorted_gate = gates.reshape(-1)[sorted_tok * e + jnp.minimum(pair_e[order], e - 1)]
    start = jnp.cumsum(cnt) - cnt
    prow = jnp.arange(r_max, dtype=I32)
    pe = tile_expert[prow // tm]
    kk = prow - row_off[pe]
    valid = (kk < cnt[pe]) & (prow < n_valid * tm)
    src = jnp.clip(start[pe] + kk, 0, rows * k - 1)
    row_token = jnp.where(valid, sorted_tok[src], 0).astype(I32)
    row_gate = jnp.where(valid, sorted_gate[src], 0.0)[:, None]
    return row_token, row_gate, tile_expert, n_valid.reshape(1).astype(I32), dest.reshape(-1).astype(I32), n_tiles


def _moe(hf, hf_slab, l, w, *, n_valid_rows):
    rows, d = hf.shape
    k = MOE_TOP_K
    gates, chosen = _router(hf, w["moe_router_w"], w["moe_router_bias"], l, n_valid_rows=n_valid_rows)
    row_token, row_gate, tile_expert, n_valid, dest, n_tiles = _dispatch_plan(gates, chosen, k=k, tm=MOE_TILE)
    xs = _moe_gather(row_token, n_valid, hf_slab.reshape(rows, d // LANES, LANES), n_tiles=n_tiles, d=d)
    hmid = _moe_up(tile_expert, n_valid, xs, row_gate, w["moe_w_in"], l, n_tiles=n_tiles)
    ys = _moe_down(tile_expert, n_valid, hmid, w["moe_w_out"], l, n_tiles=n_tiles)
    shared = _mm(_glu(hf, w["moe_sh_in"], l, name="moe_shared_up"), w["moe_sh_out"], l, name="moe_shared_down")
    return _moe_combine(dest, ys, shared, k=k)


def kernel(x_prompt, x_sample, state_dn_conv, state_dn_rec, cache_swa_k, cache_swa_v, c_prompt, c_sample, ada_w, ada_b, norm_mix_g, norm_ffn_g, final_norm_g, dn_w_in, dn_conv_w, dn_a_log, dn_dt_bias, dn_norm_g, dn_w_out, swa_w_in, swa_b_in, swa_sinks, swa_w_out, swa_b_out, moe_router_w, moe_router_bias, moe_w_in, moe_w_out, moe_sh_in, moe_sh_out):
    w = dict(dn_w_in=dn_w_in, dn_conv_w=dn_conv_w, dn_a_log=dn_a_log, dn_dt_bias=dn_dt_bias, dn_norm_g=dn_norm_g,
             dn_w_out=dn_w_out, swa_w_in=swa_w_in, swa_b_in=swa_b_in, swa_sinks=swa_sinks, swa_w_out=swa_w_out,
             swa_b_out=swa_b_out, moe_router_w=moe_router_w, moe_router_bias=moe_router_bias, moe_w_in=moe_w_in,
             moe_w_out=moe_w_out, moe_sh_in=moe_sh_in, moe_sh_out=moe_sh_out)
    batch, seq, d = x_prompt.shape
    bs, dec_seq, _ = x_sample.shape
    assert dec_seq == 1 and seq % ROW_TILE == 0 and seq % SWA_WINDOW == 0 and seq % DN_CHUNK == 0
    depth = ada_w.shape[0]
    n_p = batch * seq
    assert n_p % bs == 0 and bs % SUBLANES == 0
    np_rows = _round_up(n_p + bs, ROW_TILE)
    rs = np_rows - n_p
    dims = dict(n_p=n_p, seq=seq, batch=batch, bs=bs, np_rows=np_rows)
    past_len = PAST_LEN

    x = jnp.concatenate([x_prompt.reshape(n_p, d), x_sample.reshape(bs, d), jnp.zeros((rs - bs, d), F32)], axis=0)
    cp = _round_up(batch + bs, 16)
    c_all = jnp.concatenate([c_prompt, c_sample, jnp.zeros((cp - batch - bs, d), F32)], axis=0)

    norm_g = jnp.stack([norm_mix_g, norm_ffn_g], axis=1).reshape(2 * depth, d)
    final_g = final_norm_g.reshape(1, d)

    def modulation(l):
        mod = _mm(c_all, ada_w, l, ada_b, lhs_silu=True, name="ada_mod")
        modp = mod[:batch].reshape(batch, 6, 1, d)
        modr = jnp.pad(mod[batch:batch + bs], ((0, rs - bs), (0, 0))).reshape(rs, 6, d).transpose(1, 0, 2)
        return modp, modr

    rn = functools.partial(_rownorm, n_prompt_rows=n_p, rows_per_seq=seq)
    p_conv, p_rec, p_k, p_v, s_conv, s_rec, s_k, s_v = [], [], [], [], [], [], [], []
    modp, modr = modulation(0)
    (hm,) = rn(x, norm_g, 0, modp, modr, mod=(MOD_SC_M, MOD_SH_M), name="norm_mix")
    y = None
    for l in range(depth):
        j = l // 2
        if l % 2 == 0:
            mix, cv_p, st_p, cv_s, st_s = _deltanet(hm, j, w, state_dn_conv, state_dn_rec, dims)
            p_conv.append(cv_p), p_rec.append(st_p), s_conv.append(cv_s), s_rec.append(st_s)
        else:
            mix, kk_p, vv_p, kk_s, vv_s = _swa(hm, j, w, cache_swa_k, cache_swa_v, dims, past_len)
            p_k.append(kk_p), p_v.append(vv_p), s_k.append(kk_s), s_v.append(vv_s)
        x, hf, hf_slab = rn(x, norm_g, 2 * l + 1, modp, modr, resid=(mix, MOD_GT_M), mod=(MOD_SC_F, MOD_SH_F),
                            out_x=True, out_slab=True, name="resid_norm_ffn")
        moe = _moe(hf, hf_slab, l, w, n_valid_rows=n_p + bs)
        if l + 1 < depth:
            gt_mods = (modp, modr)
            modp, modr = modulation(l + 1)
            x, hm = _resid_then_norm(x, moe, gt_mods, norm_g, 2 * (l + 1), (modp, modr), n_p, seq)
        else:
            (y,) = rn(x, final_g, 0, modp, modr, resid=(moe, MOD_GT_F), h_dtype=F32, name="final_norm")

    y_prompt = y[:n_p].reshape(batch, seq, d)
    y_sample = y[n_p:n_p + bs].reshape(bs, 1, d)
    return (y_prompt, y_sample, jnp.stack(p_conv), jnp.stack(p_rec), jnp.stack(p_k), jnp.stack(p_v),
            jnp.stack(s_conv), jnp.stack(s_rec), jnp.stack(s_k), jnp.stack(s_v))


def _resid_then_norm(x, y, gt_mods, norm_g, g_row, next_mods, n_p, seq):
    modp, modr = gt_mods
    nmodp, nmodr = next_mods
    modp2 = jnp.concatenate([modp[:, MOD_GT_F:MOD_GT_F + 1], nmodp[:, MOD_SC_M:MOD_SC_M + 1], nmodp[:, MOD_SH_M:MOD_SH_M + 1]], axis=1)
    modr2 = jnp.stack([modr[MOD_GT_F], nmodr[MOD_SC_M], nmodr[MOD_SH_M]], axis=0)
    return _rownorm(x, norm_g, g_row, modp2, modr2, n_prompt_rows=n_p, rows_per_seq=seq, resid=(y, 0), mod=(1, 2),
                    out_x=True, name="resid_norm_mix")
```

```python
import functools

import jax
import jax.numpy as jnp
from jax import lax
from jax.experimental import pallas as pl
from jax.experimental.pallas import tpu as pltpu

F32 = jnp.float32
BF16 = jnp.bfloat16
I32 = jnp.int32

NORM_EPS = 1e-6
ROPE_THETA = 10000.0
ROUTED_SCALE = 2.5
MOE_GROUPS = 8
MOE_TOPK_GROUPS = 4
MOE_TOP_K = 8
SWA_WINDOW = 128
DN_CHUNK = 64
DN_SOLVE_BLOCK = 16

LANES = 128
SUBLANES = 8
VMEM_LIMIT = 56 * 1024 * 1024

ROW_TILE = 256
NORM_TILE = 128
MOE_TILE = 256
COMBINE_TILE = 64
RANK_BITS = 20
PLAN_CHUNK = 2048
DMA_UNROLL = 8
PAST_LEN = 16384
DN_HEADS_PER_STEP = 8
DN_HEADS_PER_MATMUL = 4


def _cparams(sem, vmem=None, bounds_checks=True):
    return pltpu.CompilerParams(dimension_semantics=sem, vmem_limit_bytes=vmem, disable_bounds_checks=not bounds_checks)


def _round_up(x, m):
    return (x + m - 1) // m * m


def _pick_tile(n, pref, mult=SUBLANES):
    if n <= pref:
        return n
    for t in range(pref - pref % mult, 0, -mult):
        if n % t == 0:
            return t
    return n


def _silu(x):
    return x * jax.nn.sigmoid(x)


def _bdot(a, b):
    return jnp.dot(a.astype(BF16), b.astype(BF16), preferred_element_type=F32)


def _bdot_nt(a, b):
    return lax.dot_general(a.astype(BF16), b.astype(BF16), (((1,), (1,)), ((), ())), preferred_element_type=F32)


def _bdot_tn(a, b):
    return lax.dot_general(a.astype(BF16), b.astype(BF16), (((0,), (0,)), ((), ())), preferred_element_type=F32)


def _split3(x):
    h1 = x.astype(BF16)
    r1 = x - h1.astype(F32)
    h2 = r1.astype(BF16)
    h3 = (r1 - h2.astype(F32)).astype(BF16)
    return h1, h2, h3


def _mm_kernel(a_ref, w_ref, *rest, lhs_silu, has_bias):
    if has_bias:
        b_ref, o_ref, wc_ref = rest
    else:
        o_ref, wc_ref = rest

    @pl.when(pl.program_id(1) == 0)
    def _():
        wc_ref[...] = w_ref[...].astype(BF16)

    a = a_ref[...]
    if lhs_silu:
        a = _silu(a.astype(F32))
    acc = jnp.dot(a.astype(BF16), wc_ref[...], preferred_element_type=F32)
    if has_bias:
        acc = acc + b_ref[...]
    o_ref[...] = acc.astype(o_ref.dtype)


def _mm(a, w, l, bias=None, *, cols=None, out_dtype=F32, lhs_silu=False, tm_pref=768, name):
    m, k = a.shape
    c0, c1 = cols if cols is not None else (0, w.shape[2])
    n = c1 - c0
    tm = _pick_tile(m, tm_pref, 16)
    tn = _pick_tile(n, 512 if k <= 4096 else 256, LANES)
    assert c0 % tn == 0 and n % tn == 0
    jb = c0 // tn
    grid = (n // tn, m // tm)
    in_specs = [
        pl.BlockSpec((tm, k), lambda j, i: (i, 0)),
        pl.BlockSpec((None, k, tn), lambda j, i: (l, 0, jb + j)),
    ]
    args = [a, w]
    if bias is not None:
        in_specs.append(pl.BlockSpec((None, 1, tn), lambda j, i: (l, 0, jb + j)))
        args.append(bias.reshape(bias.shape[0], 1, w.shape[2]))
    return pl.pallas_call(
        functools.partial(_mm_kernel, lhs_silu=lhs_silu, has_bias=bias is not None),
        grid=grid,
        in_specs=in_specs,
        out_specs=pl.BlockSpec((tm, tn), lambda j, i: (i, j)),
        out_shape=jax.ShapeDtypeStruct((m, n), out_dtype),
        scratch_shapes=[pltpu.VMEM((k, tn), BF16)],
        compiler_params=_cparams(("arbitrary", "arbitrary"), VMEM_LIMIT),
        name=name,
    )(*args)


def _glu_kernel(a_ref, wg_ref, wu_ref, o_ref, wgc_ref, wuc_ref):
    @pl.when(pl.program_id(1) == 0)
    def _():
        wgc_ref[...] = wg_ref[...].astype(BF16)
        wuc_ref[...] = wu_ref[...].astype(BF16)

    a = a_ref[...]
    g = jnp.dot(a, wgc_ref[...], preferred_element_type=F32)
    u = jnp.dot(a, wuc_ref[...], preferred_element_type=F32)
    o_ref[...] = (_silu(g) * u).astype(o_ref.dtype)


def _glu(a, w, l, *, name):
    m, k = a.shape
    h = w.shape[2] // 2
    tm = _pick_tile(m, 768, 16)
    tn = _pick_tile(h, 256, LANES)
    nj = h // tn
    return pl.pallas_call(
        _glu_kernel,
        grid=(nj, m // tm),
        in_specs=[
            pl.BlockSpec((tm, k), lambda j, i: (i, 0)),
            pl.BlockSpec((None, k, tn), lambda j, i: (l, 0, j)),
            pl.BlockSpec((None, k, tn), lambda j, i: (l, 0, nj + j)),
        ],
        out_specs=pl.BlockSpec((tm, tn), lambda j, i: (i, j)),
        out_shape=jax.ShapeDtypeStruct((m, h), BF16),
        scratch_shapes=[pltpu.VMEM((k, tn), BF16), pltpu.VMEM((k, tn), BF16)],
        compiler_params=_cparams(("arbitrary", "arbitrary"), VMEM_LIMIT),
        name=name,
    )(a, w, w)


MOD_SH_M, MOD_SC_M, MOD_GT_M, MOD_SH_F, MOD_SC_F, MOD_GT_F = range(6)


def _rownorm_kernel(*refs, n_prompt_tiles, has_resid, has_mod, out_x, out_slab, d):
    refs = list(refs)
    x_ref = refs.pop(0)
    if has_resid:
        y_ref, gtb_ref, gtr_ref = refs.pop(0), refs.pop(0), refs.pop(0)
    g_ref = refs.pop(0)
    if has_mod:
        scb_ref, shb_ref, scr_ref, shr_ref = refs.pop(0), refs.pop(0), refs.pop(0), refs.pop(0)
    outs = refs
    is_prompt = pl.program_id(0) < n_prompt_tiles

    def body(prompt):
        x = x_ref[...]
        if has_resid:
            gt = gtb_ref[...] if prompt else gtr_ref[...]
            x = x + gt * y_ref[...]
        o = list(outs)
        if out_x:
            o.pop(0)[...] = x
        h = x * lax.rsqrt(jnp.mean(x * x, axis=-1, keepdims=True) + NORM_EPS) * g_ref[...]
        if has_mod:
            sc = scb_ref[...] if prompt else scr_ref[...]
            sh = shb_ref[...] if prompt else shr_ref[...]
            h = h * (1.0 + sc) + sh
        h_ref = o.pop(0)
        h_ref[...] = h.astype(h_ref.dtype)
        if out_slab:
            slab_ref = o.pop(0)
            rows = x.shape[0]
            for s in range(d // LANES):
                slab_ref[pl.ds(s, rows, stride=d // LANES), :] = h[:, s * LANES:(s + 1) * LANES]

    @pl.when(is_prompt)
    def _():
        body(True)

    @pl.when(jnp.logical_not(is_prompt))
    def _():
        body(False)


def _rownorm(x, g, l, modp, modr, *, n_prompt_rows, rows_per_seq, resid=None, mod=None, out_x=False,
             h_dtype=BF16, out_slab=False, name):
    rows, d = x.shape
    tr = NORM_TILE
    n_ptiles = n_prompt_rows // tr
    tiles_per_seq = rows_per_seq // tr
    nb = modp.shape[0]

    def bspec(c):
        return pl.BlockSpec((None, None, 1, d), lambda i: (jnp.minimum(i // tiles_per_seq, nb - 1), c, 0, 0))

    def rspec(c):
        return pl.BlockSpec((None, tr, d), lambda i: (c, jnp.maximum(i - n_ptiles, 0), 0))

    row_spec = pl.BlockSpec((tr, d), lambda i: (i, 0))
    in_specs, args = [row_spec], [x]
    if resid is not None:
        y, gt_c = resid
        in_specs += [row_spec, bspec(gt_c), rspec(gt_c)]
        args += [y, modp, modr]
    in_specs.append(pl.BlockSpec((None, 1, d), lambda i: (l, 0, 0)))
    args.append(g.reshape(g.shape[0], 1, d))
    if mod is not None:
        sc_c, sh_c = mod
        in_specs += [bspec(sc_c), bspec(sh_c), rspec(sc_c), rspec(sh_c)]
        args += [modp, modp, modr, modr]
    out_specs, out_shape = [], []
    if out_x:
        out_specs.append(row_spec)
        out_shape.append(jax.ShapeDtypeStruct((rows, d), F32))
    out_specs.append(row_spec)
    out_shape.append(jax.ShapeDtypeStruct((rows, d), h_dtype))
    if out_slab:
        out_specs.append(pl.BlockSpec((tr * (d // LANES), LANES), lambda i: (i, 0)))
        out_shape.append(jax.ShapeDtypeStruct((rows * (d // LANES), LANES), F32))
    return pl.pallas_call(
        functools.partial(_rownorm_kernel, n_prompt_tiles=n_ptiles, has_resid=resid is not None,
                          has_mod=mod is not None, out_x=out_x, out_slab=out_slab, d=d),
        grid=(rows // tr,),
        in_specs=in_specs,
        out_specs=out_specs,
        out_shape=out_shape,
        compiler_params=_cparams(("arbitrary",), VMEM_LIMIT),
        name=name,
    )(*args)


def _l2norm_groups(y, width):
    parts = []
    for s in range(y.shape[1] // width):
        seg = y[:, s * width:(s + 1) * width]
        parts.append(seg * lax.rsqrt(jnp.sum(seg * seg, axis=-1, keepdims=True) + NORM_EPS))
    return parts


def _dn_conv_kernel(x_ref, w_ref, o_ref, xs_ref, *, tiles_per_seq, n_norm_tiles, dk):
    tr = x_ref.shape[0]
    j = pl.program_id(0)
    first = (pl.program_id(1) % tiles_per_seq) == 0

    @pl.when(first)
    def _():
        xs_ref[0:SUBLANES, :] = jnp.zeros((SUBLANES, xs_ref.shape[1]), F32)

    @pl.when(jnp.logical_not(first))
    def _():
        xs_ref[0:SUBLANES, :] = xs_ref[tr:tr + SUBLANES, :]

    xs_ref[SUBLANES:SUBLANES + tr, :] = x_ref[...]
    w = w_ref[...]
    nw = w.shape[0]
    acc = xs_ref[SUBLANES:SUBLANES + tr, :] * w[nw - 1:nw, :]
    for t in range(1, nw):
        acc = acc + xs_ref[SUBLANES - t:SUBLANES - t + tr, :] * w[nw - 1 - t:nw - t, :]
    y = _silu(acc)

    @pl.when(j < n_norm_tiles)
    def _():
        for s, seg in enumerate(_l2norm_groups(y, dk)):
            o_ref[:, s * dk:(s + 1) * dk] = seg

    @pl.when(j >= n_norm_tiles)
    def _():
        o_ref[...] = y


def _dn_conv_prompt(p, conv_w, l, *, n_rows, rows_per_seq, conv_ch, qk_dim, dk):
    tr = ROW_TILE
    tc = _pick_tile(qk_dim, 512, dk)
    return pl.pallas_call(
        functools.partial(_dn_conv_kernel, tiles_per_seq=rows_per_seq // tr, n_norm_tiles=2 * qk_dim // tc, dk=dk),
        grid=(conv_ch // tc, n_rows // tr),
        in_specs=[
            pl.BlockSpec((tr, tc), lambda j, i: (i, j)),
            pl.BlockSpec((None, conv_w.shape[1], tc), lambda j, i: (l, 0, j)),
        ],
        out_specs=pl.BlockSpec((tr, tc), lambda j, i: (i, j)),
        out_shape=jax.ShapeDtypeStruct((n_rows, conv_ch), F32),
        scratch_shapes=[pltpu.VMEM((tr + SUBLANES, tc), F32)],
        compiler_params=_cparams(("arbitrary", "arbitrary"), VMEM_LIMIT),
        name="dn_conv_prompt",
    )(p, conv_w)


def _dn_conv_step_kernel(x_ref, prev_ref, w_ref, o_ref, *, n_norm_tiles, dk):
    w = w_ref[...]
    nw = w.shape[0]
    acc = x_ref[...] * w[nw - 1:nw, :]
    for t in range(nw - 1):
        acc = acc + prev_ref[t] * w[t:t + 1, :]
    y = _silu(acc)

    @pl.when(pl.program_id(0) < n_norm_tiles)
    def _():
        for s, seg in enumerate(_l2norm_groups(y, dk)):
            o_ref[:, s * dk:(s + 1) * dk] = seg

    @pl.when(pl.program_id(0) >= n_norm_tiles)
    def _():
        o_ref[...] = y


def _dn_conv_step(p, prev, conv_w, l, *, row0, rows, conv_ch, qk_dim, dk):
    tc = _pick_tile(qk_dim, 512, dk)
    rb = row0 // rows
    return pl.pallas_call(
        functools.partial(_dn_conv_step_kernel, n_norm_tiles=2 * qk_dim // tc, dk=dk),
        grid=(conv_ch // tc,),
        in_specs=[
            pl.BlockSpec((rows, tc), lambda j: (rb, j)),
            pl.BlockSpec((prev.shape[0], rows, tc), lambda j: (0, 0, j)),
            pl.BlockSpec((None, conv_w.shape[1], tc), lambda j: (l, 0, j)),
        ],
        out_specs=pl.BlockSpec((rows, tc), lambda j: (0, j)),
        out_shape=jax.ShapeDtypeStruct((rows, conv_ch), F32),
        compiler_params=_cparams(("arbitrary",), VMEM_LIMIT),
        name="dn_conv_step",
    )(p, prev, conv_w)


def _dn_gate_kernel(ba_ref, alog_ref, dtb_ref, beta_ref, g_ref, gc_ref, *, hv, chunk):
    ba = ba_ref[...]
    beta_ref[...] = jax.nn.sigmoid(ba[:, :hv])
    x = ba[:, hv:] + dtb_ref[...]
    g = -jnp.exp(alog_ref[...]) * (jnp.maximum(x, 0.0) + jnp.log1p(jnp.exp(-jnp.abs(x))))
    g_ref[...] = g
    tril = (lax.broadcasted_iota(I32, (chunk, chunk), 0) >= lax.broadcasted_iota(I32, (chunk, chunk), 1)).astype(BF16)
    for c in range(ba.shape[0] // chunk):
        h1, h2, h3 = _split3(g[c * chunk:(c + 1) * chunk, :])
        gc = (jnp.dot(tril, h1, preferred_element_type=F32) + jnp.dot(tril, h2, preferred_element_type=F32)
              + jnp.dot(tril, h3, preferred_element_type=F32))
        gc_ref[c * chunk:(c + 1) * chunk, :] = gc


def _dn_gates(ba, a_log, dt_bias, l, *, hv):
    rows = ba.shape[0]
    tr = ROW_TILE
    spec = pl.BlockSpec((tr, hv), lambda i: (i, 0))
    pspec = pl.BlockSpec((None, 1, hv), lambda i: (l, 0, 0))
    shp = jax.ShapeDtypeStruct((rows, hv), F32)
    return pl.pallas_call(
        functools.partial(_dn_gate_kernel, hv=hv, chunk=DN_CHUNK),
        grid=(rows // tr,),
        in_specs=[pl.BlockSpec((tr, 2 * hv), lambda i: (i, 0)), pspec, pspec],
        out_specs=[spec, spec, spec],
        out_shape=[shp, shp, shp],
        compiler_params=_cparams(("arbitrary",)),
        name="dn_gates",
    )(ba, a_log.reshape(a_log.shape[0], 1, hv), dt_bias.reshape(dt_bias.shape[0], 1, hv))


def _unit_lower_solve(lmat, rhs, eye_f, blk, n_outer):
    n = lmat.shape[0]
    ld = jnp.where(blk, lmat, 0.0)
    lo = lmat - ld
    p = -ld
    dinv = eye_f + p
    span = 2
    while span < DN_SOLVE_BLOCK:
        if span == 2:
            p = _bdot(p, p)
        if 2 * span < DN_SOLVE_BLOCK:
            r = _bdot(p, jnp.concatenate([dinv, p], axis=1))
            dinv = dinv + r[:, :n]
            p = r[:, n:]
        else:
            dinv = dinv + _bdot(p, dinv)
        span *= 2
    r = _bdot(dinv, jnp.concatenate([lo, rhs], axis=1))
    nmat = -r[:, :n]
    y0 = r[:, n:]
    y = y0
    for _ in range(n_outer - 1):
        y = y0 + _bdot(nmat, y)
    return y


def _gated_rmsnorm(o, z, g):
    y = o * lax.rsqrt(jnp.mean(o * o, axis=-1, keepdims=True) + NORM_EPS) * g
    return y * _silu(z)


def _dn_chunk_kernel(q_ref, k_ref, v_ref, z_ref, beta_ref, gc_ref, gl_ref, ng_ref, og_ref, sout_ref, s_ref,
                     *, heads, rep, dk, dv):
    c = q_ref.shape[0]
    ci = pl.program_id(2)

    @pl.when(ci == 0)
    def _():
        s_ref[...] = jnp.zeros(s_ref.shape, F32)

    pack = min(DN_HEADS_PER_MATMUL, heads)
    n = pack * c
    row = lax.broadcasted_iota(I32, (n, n), 0)
    col = lax.broadcasted_iota(I32, (n, n), 1)
    eye = row == col
    eye_f = eye.astype(F32)
    same_head = (row // c) == (col // c)
    causal = same_head & (row >= col)
    strict = same_head & (row > col)
    blk = (row // DN_SOLVE_BLOCK) == (col // DN_SOLVE_BLOCK)
    own_state = (lax.broadcasted_iota(I32, (n, pack * dv), 0) // c) == (lax.broadcasted_iota(I32, (n, pack * dv), 1) // dv)
    ng = ng_ref[...]
    beta = beta_ref[...]
    gc = gc_ref[...]
    gl = gl_ref[...]

    def rows(fn):
        return jnp.concatenate([fn(s) for s in hs], axis=0)

    for s0 in range(0, heads, pack):
        hs = range(s0, s0 + pack)
        qst = rows(lambda s: q_ref[:, (s // rep) * dk:(s // rep + 1) * dk]) * (dk ** -0.5)
        kst = rows(lambda s: k_ref[:, (s // rep) * dk:(s // rep + 1) * dk])
        vst = rows(lambda s: v_ref[:, s * dv:(s + 1) * dv])
        zst = rows(lambda s: z_ref[:, s * dv:(s + 1) * dv])
        bcol = rows(lambda s: beta[:, s:s + 1])
        gcol = rows(lambda s: gc[:, s:s + 1])
        glcol = rows(lambda s: jnp.broadcast_to(gl[:, s:s + 1], (c, 1)))
        grow = jnp.sum(jnp.where(eye, gcol, 0.0), axis=0, keepdims=True)
        decay = jnp.exp(jnp.where(causal, gcol - grow, -jnp.inf))
        egc = jnp.exp(gcol)
        kb = kst * bcol
        kq = _bdot_nt(jnp.concatenate([kb, qst], axis=0), kst)
        lmat = jnp.where(strict, kq[:n] * decay, 0.0)
        attn = kq[n:] * decay
        sol = _unit_lower_solve(lmat, jnp.concatenate([vst * bcol, kb * egc], axis=1), eye_f, blk, c // DN_SOLVE_BLOCK)
        u = sol[:, :dv]
        w = sol[:, dv:]
        st = s_ref[:, s0 * dv:(s0 + pack) * dv]
        wq = _bdot(jnp.concatenate([w, qst * egc], axis=0), st)

        def own_cols(x):
            return jnp.concatenate([x[h * c:(h + 1) * c, h * dv:(h + 1) * dv] for h in range(pack)], axis=0)

        v_new = u - own_cols(wq[:n])
        o = own_cols(wq[n:]) + _bdot(attn, v_new)
        kd = kst * jnp.exp(glcol - gcol)
        v_bd = jnp.where(own_state, jnp.concatenate([v_new] * pack, axis=1), 0.0)
        eg = jnp.concatenate([jnp.broadcast_to(jnp.exp(gl[:, s:s + 1]), (1, dv)) for s in hs], axis=1)
        s_ref[:, s0 * dv:(s0 + pack) * dv] = st * eg + _bdot_tn(kd, v_bd)
        og = _gated_rmsnorm(o, zst, ng).astype(og_ref.dtype)
        for h, s in enumerate(hs):
            og_ref[:, s * dv:(s + 1) * dv] = og[h * c:(h + 1) * c]

    @pl.when(ci == pl.num_programs(2) - 1)
    def _():
        for s in range(heads):
            sout_ref[s] = s_ref[:, s * dv:(s + 1) * dv]


def _dn_chunks(qkv, p, beta3, gc3, gl4, norm_g, l, *, batch, seq, hk, hv, dk, dv, conv_ch):
    c = DN_CHUNK
    hb = min(DN_HEADS_PER_STEP, hv)
    rep = hv // hk
    nc = seq // c
    qw = (hb // rep) * dk
    vw = hb * dv
    qk_dim = hk * dk

    def rowmap(off):
        return lambda b, g, i: (b * nc + i, off + g)

    return pl.pallas_call(
        functools.partial(_dn_chunk_kernel, heads=hb, rep=rep, dk=dk, dv=dv),
        grid=(batch, hv // hb, nc),
        in_specs=[
            pl.BlockSpec((c, qw), rowmap(0)),
            pl.BlockSpec((c, qw), rowmap(qk_dim // qw)),
            pl.BlockSpec((c, vw), rowmap(2 * qk_dim // vw)),
            pl.BlockSpec((c, vw), rowmap(conv_ch // vw)),
            pl.BlockSpec((None, c, hb), lambda b, g, i: (g, b * nc + i, 0)),
            pl.BlockSpec((None, c, hb), lambda b, g, i: (g, b * nc + i, 0)),
            pl.BlockSpec((None, None, 1, hb), lambda b, g, i: (g, b * nc + i, 0, 0)),
            pl.BlockSpec((None, 1, dv), lambda b, g, i: (l, 0, 0)),
        ],
        out_specs=[
            pl.BlockSpec((c, vw), lambda b, g, i: (b * nc + i, g)),
            pl.BlockSpec((None, hb, dk, dv), lambda b, g, i: (b, g, 0, 0)),
        ],
        out_shape=[
            jax.ShapeDtypeStruct((batch * seq, hv * dv), BF16),
            jax.ShapeDtypeStruct((batch, hv, dk, dv), F32),
        ],
        scratch_shapes=[pltpu.VMEM((dk, hb * dv), F32)],
        compiler_params=_cparams(("arbitrary", "arbitrary", "arbitrary"), VMEM_LIMIT),
        name="dn_chunks",
    )(qkv, qkv, qkv, p, beta3, gc3, gl4, norm_g.reshape(norm_g.shape[0], 1, dv))


def _dn_step_kernel(q_ref, k_ref, v_ref, z_ref, beta_ref, g_ref, ng_ref, sin_ref, og_ref, sout_ref,
                    *, heads, rep, dk, dv):
    eye = lax.broadcasted_iota(I32, (dk, dk), 0) == lax.broadcasted_iota(I32, (dk, dk), 1)
    ng = ng_ref[...]
    beta = beta_ref[...]
    g = g_ref[...]

    def column(r):
        return jnp.sum(jnp.where(eye, r, 0.0), axis=1, keepdims=True)

    for s in range(heads):
        hq = s // rep
        q_col = column(q_ref[:, hq * dk:(hq + 1) * dk])
        k_col = column(k_ref[:, hq * dk:(hq + 1) * dk])
        v_row = v_ref[:, s * dv:(s + 1) * dv]
        a = jnp.exp(g[:, s:s + 1])
        st = sin_ref[s]
        ks = jnp.sum(st * k_col, axis=0, keepdims=True)
        v_new = beta[:, s:s + 1] * (v_row - a * ks)
        s_new = a * st + k_col * v_new
        sout_ref[s] = s_new
        o = jnp.sum(s_new * q_col, axis=0, keepdims=True) * (dk ** -0.5)
        og_ref[:, s * dv:(s + 1) * dv] = _gated_rmsnorm(o, z_ref[:, s * dv:(s + 1) * dv], ng)


def _dn_step(qkv3, z3, beta4, g4, norm_g, l, s_in, *, hk, hv, dk, dv):
    bs = s_in.shape[0]
    hb = min(DN_HEADS_PER_STEP, hv)
    rep = hv // hk
    qw = (hb // rep) * dk
    vw = hb * dv
    qk_dim = hk * dk
    return pl.pallas_call(
        functools.partial(_dn_step_kernel, heads=hb, rep=rep, dk=dk, dv=dv),
        grid=(bs, hv // hb),
        in_specs=[
            pl.BlockSpec((None, 1, qw), lambda b, g: (b, 0, g)),
            pl.BlockSpec((None, 1, qw), lambda b, g: (b, 0, qk_dim // qw + g)),
            pl.BlockSpec((None, 1, vw), lambda b, g: (b, 0, 2 * qk_dim // vw + g)),
            pl.BlockSpec((None, 1, vw), lambda b, g: (b, 0, g)),
            pl.BlockSpec((None, None, 1, hb), lambda b, g: (g, b, 0, 0)),
            pl.BlockSpec((None, None, 1, hb), lambda b, g: (g, b, 0, 0)),
            pl.BlockSpec((None, 1, dv), lambda b, g: (l, 0, 0)),
            pl.BlockSpec((None, hb, dk, dv), lambda b, g: (b, g, 0, 0)),
        ],
        out_specs=[
            pl.BlockSpec((None, 1, vw), lambda b, g: (b, 0, g)),
            pl.BlockSpec((None, hb, dk, dv), lambda b, g: (b, g, 0, 0)),
        ],
        out_shape=[
            jax.ShapeDtypeStruct((bs, 1, hv * dv), F32),
            jax.ShapeDtypeStruct(s_in.shape, F32),
        ],
        compiler_params=_cparams(("arbitrary", "arbitrary"), VMEM_LIMIT),
        name="dn_step",
    )(qkv3, qkv3, qkv3, z3, beta4, g4, norm_g.reshape(norm_g.shape[0], 1, dv), s_in)


def _head_groups(x, hb):
    rows, h = x.shape
    return x.reshape(rows, h // hb, hb).transpose(1, 0, 2)


def _deltanet(hm, j, w, st_conv, st_rec, dims):
    n_p, seq, batch, bs, np_rows = dims["n_p"], dims["seq"], dims["batch"], dims["bs"], dims["np_rows"]
    dk, dv = st_rec.shape[3], st_rec.shape[4]
    hv = st_rec.shape[2]
    conv_ch = st_conv.shape[3]
    v_dim = hv * dv
    qk_dim = (conv_ch - v_dim) // 2
    hk = qk_dim // dk
    hb = min(DN_HEADS_PER_STEP, hv)

    dn_in = w["dn_w_in"].shape[2]
    p = _mm(hm, w["dn_w_in"], j, cols=(0, conv_ch + v_dim), name="dn_in_proj")
    ba = _mm(hm, w["dn_w_in"], j, cols=(conv_ch + v_dim, dn_in), name="dn_in_proj_gates")
    beta, g, gc = _dn_gates(ba, w["dn_a_log"], w["dn_dt_bias"], j, hv=hv)

    qkv = _dn_conv_prompt(p, w["dn_conv_w"], j, n_rows=n_p, rows_per_seq=seq, conv_ch=conv_ch, qk_dim=qk_dim, dk=dk)
    gl = gc[DN_CHUNK - 1:n_p:DN_CHUNK]
    og_p, rec_p = _dn_chunks(
        qkv, p, _head_groups(beta[:n_p], hb), _head_groups(gc[:n_p], hb),
        _head_groups(gl, hb)[:, :, None, :], w["dn_norm_g"], j,
        batch=batch, seq=seq, hk=hk, hv=hv, dk=dk, dv=dv, conv_ch=conv_ch)
    nw = w["dn_conv_w"].shape[1]
    conv_p = p[:n_p, :conv_ch].reshape(batch, seq, conv_ch)[:, seq - (nw - 1):]

    prev = jnp.transpose(st_conv[j], (1, 0, 2))
    qkv_s = _dn_conv_step(p, prev, w["dn_conv_w"], j, row0=n_p, rows=bs, conv_ch=conv_ch, qk_dim=qk_dim, dk=dk)
    z_s = p[n_p:n_p + bs, conv_ch:conv_ch + v_dim]
    og_s, rec_s = _dn_step(
        qkv_s[:, None, :], z_s[:, None, :], _head_groups(beta[n_p:n_p + bs], hb)[:, :, None, :],
        _head_groups(g[n_p:n_p + bs], hb)[:, :, None, :], w["dn_norm_g"], j, st_rec[j],
        hk=hk, hv=hv, dk=dk, dv=dv)
    conv_s = jnp.concatenate([st_conv[j][:, 1:], p[n_p:n_p + bs, None, :conv_ch]], axis=1)

    og = jnp.concatenate(
        [og_p, og_s.reshape(bs, v_dim).astype(BF16), jnp.zeros((np_rows - n_p - bs, v_dim), BF16)], axis=0)
    mix = _mm(og, w["dn_w_out"], j, name="dn_out_proj")
    return mix, conv_p, rec_p, conv_s, rec_s


def _rope_kernel(x_ref, cos_ref, sin_ref, o_ref, *, hd):
    cos = cos_ref[...]
    sin = sin_ref[...]
    lane = lax.broadcasted_iota(I32, cos.shape, 1)
    first_half = (lane % hd) < hd // 2
    for s in range(x_ref.shape[1] // LANES):
        x = x_ref[:, s * LANES:(s + 1) * LANES]
        rot = jnp.where(first_half, pltpu.roll(x, LANES - hd // 2, axis=1), pltpu.roll(x, hd // 2, axis=1))
        o_ref[:, s * LANES:(s + 1) * LANES] = x * cos + rot * sin


def _rope(p2, cos, sin, *, width, hd):
    rows = p2.shape[0]
    tr = ROW_TILE
    tc = _pick_tile(width, 512, LANES)
    return pl.pallas_call(
        functools.partial(_rope_kernel, hd=hd),
        grid=(rows // tr, width // tc),
        in_specs=[
            pl.BlockSpec((tr, tc), lambda i, j: (i, j)),
            pl.BlockSpec((tr, LANES), lambda i, j: (i, 0)),
            pl.BlockSpec((tr, LANES), lambda i, j: (i, 0)),
        ],
        out_specs=pl.BlockSpec((tr, tc), lambda i, j: (i, j)),
        out_shape=jax.ShapeDtypeStruct((rows, width), F32),
        compiler_params=_cparams(("arbitrary", "arbitrary")),
        name="swa_rope",
    )(p2, cos, sin)


def _swa_prompt_kernel(q_ref, kp_ref, ko_ref, vp_ref, vo_ref, sink_ref, o_ref, *, kv, group, hd, window):
    blk = q_ref.shape[0]
    i = pl.program_id(1)
    rows = group * blk
    r = lax.broadcasted_iota(I32, (rows, 2 * blk), 0) % blk
    c = lax.broadcasted_iota(I32, (rows, 2 * blk), 1)
    mask = (c - blk <= r) & (c - blk > r - window) & ((i > 0) | (c >= blk))
    sinks = sink_ref[...]
    scale = hd ** -0.5
    for n in range(kv):
        hs = range(n * group, (n + 1) * group)
        kk = jnp.concatenate([kp_ref[:, n * hd:(n + 1) * hd], ko_ref[:, n * hd:(n + 1) * hd]], axis=0).astype(BF16)
        vv = jnp.concatenate([vp_ref[:, n * hd:(n + 1) * hd], vo_ref[:, n * hd:(n + 1) * hd]], axis=0).astype(BF16)
        qs = jnp.concatenate([q_ref[:, h * hd:(h + 1) * hd] for h in hs], axis=0)
        sink = jnp.concatenate([jnp.broadcast_to(sinks[:, h:h + 1], (blk, 1)) for h in hs], axis=0)
        s = jnp.where(mask, _bdot_nt(qs, kk) * scale, -jnp.inf)
        m = jnp.maximum(jnp.max(s, axis=-1, keepdims=True), sink)
        e = jnp.exp(s - m)
        pn = e / (jnp.sum(e, axis=-1, keepdims=True) + jnp.exp(sink - m))
        o = jnp.dot(pn.astype(BF16), vv, preferred_element_type=F32).astype(o_ref.dtype)
        for gq, h in enumerate(hs):
            o_ref[:, h * hd:(h + 1) * hd] = o[gq * blk:(gq + 1) * blk]


def _swa_prompt(rot, p2, sinks, l, *, batch, seq, heads, kv, hd):
    blk = SWA_WINDOW
    nb = seq // blk
    qw = heads * hd
    kw = kv * hd
    kcol = qw // kw

    def own(off):
        return lambda b, i: (b * nb + i, off)

    def prev(off):
        return lambda b, i: (b * nb + jnp.maximum(i - 1, 0), off)

    return pl.pallas_call(
        functools.partial(_swa_prompt_kernel, kv=kv, group=heads // kv, hd=hd, window=SWA_WINDOW),
        grid=(batch, nb),
        in_specs=[
            pl.BlockSpec((blk, qw), own(0)),
            pl.BlockSpec((blk, kw), prev(kcol)),
            pl.BlockSpec((blk, kw), own(kcol)),
            pl.BlockSpec((blk, kw), prev(kcol + 1)),
            pl.BlockSpec((blk, kw), own(kcol + 1)),
            pl.BlockSpec((None, 1, heads), lambda b, i: (l, 0, 0)),
        ],
        out_specs=pl.BlockSpec((blk, qw), own(0)),
        out_shape=jax.ShapeDtypeStruct((batch * seq, qw), BF16),
        compiler_params=_cparams(("arbitrary", "arbitrary"), VMEM_LIMIT),
        name="swa_prompt_attn",
    )(rot, rot, rot, p2, p2, sinks.reshape(sinks.shape[0], 1, heads))


def _swa_step_kernel(q_ref, kn_ref, vn_ref, ck_ref, cv_ref, sink_ref, o_ref, *, kv, group, hd, first_valid):
    buf = ck_ref.shape[0]
    valid = lax.broadcasted_iota(I32, (group, buf), 1) >= first_valid
    scale = hd ** -0.5
    for n in range(kv):
        qg = q_ref[n * group:(n + 1) * group, :]
        kc = ck_ref[:, n * hd:(n + 1) * hd]
        vc = cv_ref[:, n * hd:(n + 1) * hd]
        kn = kn_ref[n:n + 1, :]
        vn = vn_ref[n:n + 1, :]
        sink = sink_ref[n * group:(n + 1) * group, :]
        s = jnp.where(valid, _bdot_nt(qg, kc) * scale, -jnp.inf)
        s_new = jnp.sum(qg.astype(BF16).astype(F32) * kn.astype(BF16).astype(F32), axis=-1, keepdims=True) * scale
        m = jnp.maximum(jnp.maximum(jnp.max(s, axis=-1, keepdims=True), s_new), sink)
        e = jnp.exp(s - m)
        e_new = jnp.exp(s_new - m)
        den = jnp.sum(e, axis=-1, keepdims=True) + e_new + jnp.exp(sink - m)
        o = _bdot(e / den, vc) + (e_new / den) * vn
        o_ref[n * group:(n + 1) * group, :] = o


def _swa_step(q3, kn3, vn3, ck, cv, sinks, l, *, heads, kv, hd):
    bs, buf = ck.shape[0], ck.shape[1]
    kw = kv * hd
    first_valid = max(buf - SWA_WINDOW + 1, 0)
    return pl.pallas_call(
        functools.partial(_swa_step_kernel, kv=kv, group=heads // kv, hd=hd, first_valid=first_valid),
        grid=(bs,),
        in_specs=[
            pl.BlockSpec((None, heads, hd), lambda b: (b, 0, 0)),
            pl.BlockSpec((None, kv, hd), lambda b: (b, 0, 0)),
            pl.BlockSpec((None, kv, hd), lambda b: (b, 0, 0)),
            pl.BlockSpec((None, buf, kw), lambda b: (b, 0, 0)),
            pl.BlockSpec((None, buf, kw), lambda b: (b, 0, 0)),
            pl.BlockSpec((None, heads, 1), lambda b: (l, 0, 0)),
        ],
        out_specs=pl.BlockSpec((None, heads, hd), lambda b: (b, 0, 0)),
        out_shape=jax.ShapeDtypeStruct((bs, heads, hd), F32),
        compiler_params=_cparams(("arbitrary",)),
        name="swa_step_attn",
    )(q3, kn3, vn3, ck, cv, sinks.reshape(sinks.shape[0], heads, 1))


def _rope_tables(pos, hd):
    half = hd // 2
    inv = ROPE_THETA ** (-jnp.arange(half, dtype=F32) / half)
    ang = pos.astype(F32)[:, None] * inv[None, :]
    cos, sin = jnp.cos(ang), jnp.sin(ang)
    reps = LANES // hd
    return jnp.tile(jnp.concatenate([cos, cos], axis=1), (1, reps)), jnp.tile(jnp.concatenate([-sin, sin], axis=1), (1, reps))


def _swa(hm, j, w, cache_k, cache_v, dims, past_len):
    n_p, seq, batch, bs, np_rows = dims["n_p"], dims["seq"], dims["batch"], dims["bs"], dims["np_rows"]
    kv, hd = cache_k.shape[3], cache_k.shape[4]
    buf = cache_k.shape[2]
    heads = w["swa_sinks"].shape[1]
    qw, kw = heads * hd, kv * hd

    p2 = _mm(hm, w["swa_w_in"], j, w["swa_b_in"], name="swa_in_proj")
    pos = jnp.concatenate([jnp.tile(jnp.arange(seq), batch), jnp.full((np_rows - n_p,), past_len)])
    cos, sin = _rope_tables(pos, hd)
    rot = _rope(p2, cos, sin, width=qw + kw, hd=hd)

    o_p = _swa_prompt(rot, p2, w["swa_sinks"], j, batch=batch, seq=seq, heads=heads, kv=kv, hd=hd)
    keep = min(SWA_WINDOW, seq)
    k_p = rot[:n_p, qw:].reshape(batch, seq, kv, hd)[:, seq - keep:]
    v_p = p2[:n_p, qw + kw:].reshape(batch, seq, kv, hd)[:, seq - keep:]

    q_s = rot[n_p:n_p + bs, :qw].reshape(bs, heads, hd)
    kn = rot[n_p:n_p + bs, qw:].reshape(bs, kv, hd)
    vn = p2[n_p:n_p + bs, qw + kw:].reshape(bs, kv, hd)
    o_s = _swa_step(q_s, kn, vn, cache_k[j].reshape(bs, buf, kw), cache_v[j].reshape(bs, buf, kw),
                    w["swa_sinks"], j, heads=heads, kv=kv, hd=hd)
    k_s = jnp.concatenate([cache_k[j], kn[:, None]], axis=1)[:, 1:]
    v_s = jnp.concatenate([cache_v[j], vn[:, None]], axis=1)[:, 1:]

    o = jnp.concatenate([o_p, o_s.reshape(bs, qw).astype(BF16), jnp.zeros((np_rows - n_p - bs, qw), BF16)], axis=0)
    mix = _mm(o, w["swa_w_out"], j, w["swa_b_out"], name="swa_out_proj")
    return mix, k_p, v_p, k_s, v_s


def _router_kernel(h_ref, w_ref, b_ref, slot_ref, wt_ref, cnt_ref, run_ref, *, n_valid_rows, groups, topk_groups, top_k):
    tr = h_ref.shape[0]
    e = w_ref.shape[1]

    @pl.when(pl.program_id(0) == 0)
    def _():
        run_ref[...] = jnp.zeros(run_ref.shape, F32)

    h = h_ref[...]
    w = w_ref[...]
    wh = w.astype(BF16)
    wl = (w - wh.astype(F32)).astype(BF16)
    logits = jnp.dot(h, wh, preferred_element_type=F32) + jnp.dot(h, wl, preferred_element_type=F32)
    scores = jax.nn.sigmoid(logits)
    sel = scores + b_ref[...]
    lane = lax.broadcasted_iota(I32, (tr, e), 1)
    grp = lane // (e // groups)
    neg = -jnp.inf

    def first_max(v, ids, sentinel):
        m = jnp.max(v, axis=-1, keepdims=True)
        return m, jnp.min(jnp.where(v == m, ids, sentinel), axis=-1, keepdims=True)

    gscore = jnp.zeros((tr, e), F32)
    for gi in range(groups):
        v = jnp.where(grp == gi, sel, neg)
        m1, i1 = first_max(v, lane, e)
        m2 = jnp.max(jnp.where(lane == i1, neg, v), axis=-1, keepdims=True)
        gscore = jnp.where(grp == gi, m1 + m2, gscore)
    gmask = jnp.zeros((tr, e), jnp.bool_)
    for _ in range(topk_groups):
        _, gi = first_max(gscore, grp, groups)
        hit = grp == gi
        gmask = gmask | hit
        gscore = jnp.where(hit, neg, gscore)
    cur = jnp.where(gmask, sel, neg)
    chosen = jnp.zeros((tr, e), jnp.bool_)
    picks = []
    for _ in range(top_k):
        _, ei = first_max(cur, lane, e)
        hit = lane == ei
        picks.append((ei, hit))
        chosen = chosen | hit
        cur = jnp.where(hit, neg, cur)
    wts = jnp.where(chosen, scores, 0.0)
    gates = wts / (jnp.sum(wts, axis=-1, keepdims=True) + 1e-20) * ROUTED_SCALE
    real = (pl.program_id(0) * tr + lax.broadcasted_iota(I32, (tr, e), 0)) < n_valid_rows
    chosen_f = jnp.where(chosen & real, 1.0, 0.0)
    before = lax.broadcasted_iota(I32, (tr, tr), 0) > lax.broadcasted_iota(I32, (tr, tr), 1)
    rank = run_ref[...] + jnp.dot(before.astype(BF16), chosen_f.astype(BF16), preferred_element_type=F32)
    run_ref[...] = run_ref[...] + jnp.sum(chosen_f, axis=0, keepdims=True)
    cnt_ref[...] = run_ref[...].astype(I32)
    real_row = real[:, :1]
    for s, (ei, hit) in enumerate(picks):
        r = jnp.sum(jnp.where(hit, rank, 0.0), axis=-1, keepdims=True).astype(I32)
        slot_ref[:, s:s + 1] = jnp.where(real_row, ei * (1 << RANK_BITS) + r, e * (1 << RANK_BITS))
        wt_ref[:, s:s + 1] = jnp.where(real_row, jnp.sum(jnp.where(hit, gates, 0.0), axis=-1, keepdims=True), 0.0)


def _router(hf, router_w, router_b, l, *, n_valid_rows):
    rows, d = hf.shape
    e = router_w.shape[2]
    k = MOE_TOP_K
    tr = ROW_TILE
    return pl.pallas_call(
        functools.partial(_router_kernel, n_valid_rows=n_valid_rows, groups=MOE_GROUPS,
                          topk_groups=MOE_TOPK_GROUPS, top_k=MOE_TOP_K),
        grid=(rows // tr,),
        in_specs=[
            pl.BlockSpec((tr, d), lambda i: (i, 0)),
            pl.BlockSpec((None, d, e), lambda i: (l, 0, 0)),
            pl.BlockSpec((None, 1, e), lambda i: (l, 0, 0)),
        ],
        out_specs=[pl.BlockSpec((tr, k), lambda i: (i, 0)), pl.BlockSpec((tr, k), lambda i: (i, 0)),
                   pl.BlockSpec((1, e), lambda i: (0, 0))],
        out_shape=[jax.ShapeDtypeStruct((rows, k), I32), jax.ShapeDtypeStruct((rows, k), F32),
                   jax.ShapeDtypeStruct((1, e), I32)],
        scratch_shapes=[pltpu.VMEM((1, e), F32)],
        compiler_params=_cparams(("arbitrary",)),
        name="moe_router",
    )(hf, router_w, router_b.reshape(router_b.shape[0], 1, e))


def _row_copy(src_hbm, dst_vmem, sem, token, slot, slab):
    return pltpu.make_async_copy(src_hbm.at[token], dst_vmem.at[pl.ds(pl.multiple_of(slot * slab, slab), slab), :], sem)


def _moe_gather_kernel(tok_ref, nvalid_ref, x_ref, o_ref, buf_ref, sems, *, tm, slab):
    i = pl.program_id(0)
    nv = nvalid_ref[0]

    def issue(tile, half):
        def body(r, carry):
            _row_copy(x_ref, buf_ref, sems.at[half], tok_ref[tile * tm + r], half * tm + r, slab).start()
            return carry

        lax.fori_loop(0, tm, body, 0, unroll=DMA_UNROLL)

    @pl.when((i == 0) & (nv > 0))
    def _():
        issue(0, 0)

    @pl.when(i + 1 < nv)
    def _():
        issue(i + 1, (i + 1) % 2)

    @pl.when(i < nv)
    def _():
        half = i % 2

        def drain(r, carry):
            _row_copy(x_ref, buf_ref, sems.at[half], 0, half * tm + r, slab).wait()
            return carry

        lax.fori_loop(0, tm, drain, 0, unroll=DMA_UNROLL)
        base = half * tm * slab
        for s in range(slab):
            o_ref[:, s * LANES:(s + 1) * LANES] = buf_ref[pl.ds(base + s, tm, stride=slab), :].astype(o_ref.dtype)

    @pl.when(i >= nv)
    def _():
        o_ref[...] = jnp.zeros(o_ref.shape, o_ref.dtype)


def _moe_gather(row_token, n_valid_tiles, x_slab, *, n_tiles, d):
    tm = MOE_TILE
    slab = d // LANES
    gs = pltpu.PrefetchScalarGridSpec(
        num_scalar_prefetch=2,
        grid=(n_tiles,),
        in_specs=[pl.BlockSpec(memory_space=pl.ANY)],
        out_specs=pl.BlockSpec((tm, d), lambda i, tok, nv: (i, 0)),
        scratch_shapes=[pltpu.VMEM((2 * tm * slab, LANES), F32), pltpu.SemaphoreType.DMA((2,))],
    )
    return pl.pallas_call(
        functools.partial(_moe_gather_kernel, tm=tm, slab=slab),
        grid_spec=gs,
        out_shape=jax.ShapeDtypeStruct((n_tiles * tm, d), BF16),
        compiler_params=_cparams(("arbitrary",), VMEM_LIMIT, bounds_checks=False),
        name="moe_gather",
    )(row_token, n_valid_tiles, x_slab)


def _expert_changed(te_ref, i):
    return (i == 0) | (te_ref[i] != te_ref[jnp.maximum(i - 1, 0)])


def _moe_up_kernel(te_ref, nvalid_ref, x_ref, wg_ref, wu_ref, o_ref, wgc_ref, wuc_ref):
    i = pl.program_id(1)

    @pl.when(i < nvalid_ref[0])
    def _():
        @pl.when(_expert_changed(te_ref, i))
        def _():
            wgc_ref[...] = wg_ref[...].astype(BF16)
            wuc_ref[...] = wu_ref[...].astype(BF16)

        x = x_ref[...]
        g = jnp.dot(x, wgc_ref[...], preferred_element_type=F32)
        u = jnp.dot(x, wuc_ref[...], preferred_element_type=F32)
        o_ref[...] = (_silu(g) * u).astype(o_ref.dtype)

    @pl.when(i >= nvalid_ref[0])
    def _():
        o_ref[...] = jnp.zeros(o_ref.shape, o_ref.dtype)


def _moe_up(tile_expert, n_valid_tiles, xs, w_in, l, *, n_tiles):
    tm = MOE_TILE
    d = xs.shape[1]
    de = w_in.shape[3] // 2
    tj = _pick_tile(de, 512, LANES)
    nj = de // tj

    def tile(i, nv):
        return jnp.minimum(i, nv[0] - 1)

    gs = pltpu.PrefetchScalarGridSpec(
        num_scalar_prefetch=2,
        grid=(nj, n_tiles),
        in_specs=[
            pl.BlockSpec((tm, d), lambda j, i, te, nv: (tile(i, nv), 0)),
            pl.BlockSpec((None, None, d, tj), lambda j, i, te, nv: (l, te[i], 0, j)),
            pl.BlockSpec((None, None, d, tj), lambda j, i, te, nv: (l, te[i], 0, nj + j)),
        ],
        out_specs=pl.BlockSpec((tm, tj), lambda j, i, te, nv: (i, j)),
        scratch_shapes=[pltpu.VMEM((d, tj), BF16), pltpu.VMEM((d, tj), BF16)],
    )
    return pl.pallas_call(
        _moe_up_kernel,
        grid_spec=gs,
        out_shape=jax.ShapeDtypeStruct((n_tiles * tm, de), BF16),
        compiler_params=_cparams(("arbitrary", "arbitrary"), VMEM_LIMIT),
        name="moe_up",
    )(tile_expert, n_valid_tiles, xs, w_in, w_in)


def _moe_down_kernel(te_ref, nvalid_ref, h_ref, w_ref, o_ref, wc_ref, *, tm):
    i = pl.program_id(1)

    @pl.when(i < nvalid_ref[0])
    def _():
        @pl.when(_expert_changed(te_ref, i))
        def _():
            wc_ref[...] = w_ref[...].astype(BF16)

        y = jnp.dot(h_ref[...], wc_ref[...], preferred_element_type=F32)
        for s in range(y.shape[1] // LANES):
            o_ref[:, s, :] = y[:, s * LANES:(s + 1) * LANES]

    @pl.when(i >= nvalid_ref[0])
    def _():
        o_ref[...] = jnp.zeros(o_ref.shape, o_ref.dtype)


def _moe_down(tile_expert, n_valid_tiles, h, w_out, l, *, n_tiles):
    tm = MOE_TILE
    de, d = w_out.shape[2], w_out.shape[3]
    tn = _pick_tile(d, 2048, LANES)
    nn = d // tn

    def tile(i, nv):
        return jnp.minimum(i, nv[0] - 1)

    gs = pltpu.PrefetchScalarGridSpec(
        num_scalar_prefetch=2,
        grid=(nn, n_tiles),
        in_specs=[
            pl.BlockSpec((tm, de), lambda j, i, te, nv: (tile(i, nv), 0)),
            pl.BlockSpec((None, None, de, tn), lambda j, i, te, nv: (l, te[i], 0, j)),
        ],
        out_specs=pl.BlockSpec((tm, tn // LANES, LANES), lambda j, i, te, nv: (i, j, 0)),
        scratch_shapes=[pltpu.VMEM((de, tn), BF16)],
    )
    return pl.pallas_call(
        functools.partial(_moe_down_kernel, tm=tm),
        grid_spec=gs,
        out_shape=jax.ShapeDtypeStruct((n_tiles * tm, d // LANES, LANES), F32),
        compiler_params=_cparams(("arbitrary", "arbitrary"), VMEM_LIMIT),
        name="moe_down",
    )(tile_expert, n_valid_tiles, h, w_out)


def _moe_combine_kernel(slot_ref, off_ref, ys_ref, wt_ref, sh_ref, o_ref, buf_ref, sems, *, tr, k, slab):
    i = pl.program_id(0)
    n_rows = tr * k

    def issue(step, half):
        def body(r, carry):
            p = slot_ref[step * n_rows + r]
            dest = off_ref[p >> RANK_BITS] + (p & ((1 << RANK_BITS) - 1))
            _row_copy(ys_ref, buf_ref, sems.at[half], dest, half * n_rows + r, slab).start()
            return carry

        lax.fori_loop(0, n_rows, body, 0, unroll=DMA_UNROLL)

    @pl.when(i == 0)
    def _():
        issue(0, 0)

    @pl.when(i + 1 < pl.num_programs(0))
    def _():
        issue(i + 1, (i + 1) % 2)

    half = i % 2

    def drain(r, carry):
        _row_copy(ys_ref, buf_ref, sems.at[half], 0, half * n_rows + r, slab).wait()
        return carry

    lax.fori_loop(0, n_rows, drain, 0, unroll=DMA_UNROLL)
    base = half * n_rows * slab
    wt = wt_ref[...]
    gates = [jnp.broadcast_to(wt[:, e:e + 1], (tr, LANES)) for e in range(k)]
    for s in range(slab):
        acc = sh_ref[:, s * LANES:(s + 1) * LANES]
        for e in range(k):
            acc = acc + gates[e] * buf_ref[pl.ds(base + e * slab + s, tr, stride=k * slab), :]
        o_ref[:, s * LANES:(s + 1) * LANES] = acc


def _moe_combine(slots, row_off, ys, wts, shared, *, k):
    rows, d = shared.shape
    slab = d // LANES
    tr = COMBINE_TILE
    gs = pltpu.PrefetchScalarGridSpec(
        num_scalar_prefetch=2,
        grid=(rows // tr,),
        in_specs=[
            pl.BlockSpec(memory_space=pl.ANY),
            pl.BlockSpec((tr, k), lambda i, sl, off: (i, 0)),
            pl.BlockSpec((tr, d), lambda i, sl, off: (i, 0)),
        ],
        out_specs=pl.BlockSpec((tr, d), lambda i, sl, off: (i, 0)),
        scratch_shapes=[pltpu.VMEM((2 * tr * k * slab, LANES), F32), pltpu.SemaphoreType.DMA((2,))],
    )
    return pl.pallas_call(
        functools.partial(_moe_combine_kernel, tr=tr, k=k, slab=slab),
        grid_spec=gs,
        out_shape=jax.ShapeDtypeStruct((rows, d), F32),
        compiler_params=_cparams(("arbitrary",), VMEM_LIMIT, bounds_checks=False),
        name="moe_combine",
    )(slots, row_off, ys, wts, shared)


def _moe_plan_tables(cnt_ref, te_ref, nv_ref, off_ref, *, n_experts, tm, n_tiles):
    def per_expert(e, first_tile):
        nt = (cnt_ref[e] + tm - 1) // tm
        off_ref[e] = first_tile * tm

        def fill(t, carry):
            te_ref[first_tile + t] = e
            return carry

        lax.fori_loop(0, nt, fill, 0)
        return first_tile + nt

    nv = lax.fori_loop(0, n_experts, per_expert, 0)
    off_ref[n_experts] = 0
    nv_ref[0] = nv
    last = te_ref[jnp.maximum(nv - 1, 0)]

    def fill_tail(t, carry):
        te_ref[t] = last
        return carry

    lax.fori_loop(nv, n_tiles, fill_tail, 0)


def _moe_plan_kernel(cnt_ref, slot_ref, tok_ref, te_ref, nv_ref, off_ref, *, n_experts, k, tm, n_tiles, chunk):
    i = pl.program_id(0)
    zero_steps = n_tiles * tm // chunk

    @pl.when(i == 0)
    def _():
        _moe_plan_tables(cnt_ref, te_ref, nv_ref, off_ref, n_experts=n_experts, tm=tm, n_tiles=n_tiles)

    @pl.when(i < zero_steps)
    def _():
        def zero(r, carry):
            tok_ref[i * chunk + r] = 0
            return carry

        lax.fori_loop(0, chunk, zero, 0)

    @pl.when(i >= zero_steps)
    def _():
        base = (i - zero_steps) * chunk

        def place(r, carry):
            p = slot_ref[r]
            e = p >> RANK_BITS

            @pl.when(e < n_experts)
            def _():
                tok_ref[off_ref[e] + (p & ((1 << RANK_BITS) - 1))] = (base + r) // k

            return carry

        lax.fori_loop(0, chunk, place, 0)


def _moe_plan(counts, slots, *, k, tm):
    e = counts.shape[0]
    n_tiles = slots.shape[0] // tm + e
    chunk = PLAN_CHUNK
    assert slots.shape[0] % chunk == 0 and (n_tiles * tm) % chunk == 0
    zero_steps = n_tiles * tm // chunk

    def whole(n):
        return pl.BlockSpec((n,), lambda i: (0,), memory_space=pltpu.SMEM)

    row_token, tile_expert, n_valid, row_off = pl.pallas_call(
        functools.partial(_moe_plan_kernel, n_experts=e, k=k, tm=tm, n_tiles=n_tiles, chunk=chunk),
        grid=(zero_steps + slots.shape[0] // chunk,),
        in_specs=[whole(e), pl.BlockSpec((chunk,), lambda i: (jnp.maximum(i - zero_steps, 0),), memory_space=pltpu.SMEM)],
        out_specs=[whole(n_tiles * tm), whole(n_tiles), whole(1), whole(e + 1)],
        out_shape=[jax.ShapeDtypeStruct((n_tiles * tm,), I32), jax.ShapeDtypeStruct((n_tiles,), I32),
                   jax.ShapeDtypeStruct((1,), I32), jax.ShapeDtypeStruct((e + 1,), I32)],
        compiler_params=_cparams(("arbitrary",)),
        name="moe_plan",
    )(counts, slots)
    return row_token, tile_expert, n_valid, row_off, n_tiles


def _moe(hf, hf_slab, l, w, *, n_valid_rows):
    rows, d = hf.shape
    k = MOE_TOP_K
    assert rows < (1 << RANK_BITS)
    slots, wts, counts = _router(hf, w["moe_router_w"], w["moe_router_bias"], l, n_valid_rows=n_valid_rows)
    slots = slots.reshape(rows * k)
    row_token, tile_expert, n_valid, row_off, n_tiles = _moe_plan(counts.reshape(-1), slots, k=k, tm=MOE_TILE)
    xs = _moe_gather(row_token, n_valid, hf_slab.reshape(rows, d // LANES, LANES), n_tiles=n_tiles, d=d)
    hmid = _moe_up(tile_expert, n_valid, xs, w["moe_w_in"], l, n_tiles=n_tiles)
    ys = _moe_down(tile_expert, n_valid, hmid, w["moe_w_out"], l, n_tiles=n_tiles)
    shared = _mm(_glu(hf, w["moe_sh_in"], l, name="moe_shared_up"), w["moe_sh_out"], l, name="moe_shared_down")
    return _moe_combine(slots, row_off, ys, wts, shared, k=k)


def kernel(x_prompt, x_sample, state_dn_conv, state_dn_rec, cache_swa_k, cache_swa_v, c_prompt, c_sample, ada_w, ada_b, norm_mix_g, norm_ffn_g, final_norm_g, dn_w_in, dn_conv_w, dn_a_log, dn_dt_bias, dn_norm_g, dn_w_out, swa_w_in, swa_b_in, swa_sinks, swa_w_out, swa_b_out, moe_router_w, moe_router_bias, moe_w_in, moe_w_out, moe_sh_in, moe_sh_out):
    w = dict(dn_w_in=dn_w_in, dn_conv_w=dn_conv_w, dn_a_log=dn_a_log, dn_dt_bias=dn_dt_bias, dn_norm_g=dn_norm_g,
             dn_w_out=dn_w_out, swa_w_in=swa_w_in, swa_b_in=swa_b_in, swa_sinks=swa_sinks, swa_w_out=swa_w_out,
             swa_b_out=swa_b_out, moe_router_w=moe_router_w, moe_router_bias=moe_router_bias, moe_w_in=moe_w_in,
             moe_w_out=moe_w_out, moe_sh_in=moe_sh_in, moe_sh_out=moe_sh_out)
    batch, seq, d = x_prompt.shape
    bs, dec_seq, _ = x_sample.shape
    assert dec_seq == 1 and seq % ROW_TILE == 0 and seq % SWA_WINDOW == 0 and seq % DN_CHUNK == 0
    depth = ada_w.shape[0]
    n_p = batch * seq
    assert n_p % bs == 0 and bs % SUBLANES == 0
    np_rows = _round_up(n_p + bs, ROW_TILE)
    rs = np_rows - n_p
    dims = dict(n_p=n_p, seq=seq, batch=batch, bs=bs, np_rows=np_rows)
    past_len = PAST_LEN

    x = jnp.concatenate([x_prompt.reshape(n_p, d), x_sample.reshape(bs, d), jnp.zeros((rs - bs, d), F32)], axis=0)
    cp = _round_up(batch + bs, 16)
    c_all = jnp.concatenate([c_prompt, c_sample, jnp.zeros((cp - batch - bs, d), F32)], axis=0)

    norm_g = jnp.stack([norm_mix_g, norm_ffn_g], axis=1).reshape(2 * depth, d)
    final_g = final_norm_g.reshape(1, d)

    def modulation(l):
        mod = _mm(c_all, ada_w, l, ada_b, lhs_silu=True, name="ada_mod")
        modp = mod[:batch].reshape(batch, 6, 1, d)
        modr = jnp.pad(mod[batch:batch + bs], ((0, rs - bs), (0, 0))).reshape(rs, 6, d).transpose(1, 0, 2)
        return modp, modr

    rn = functools.partial(_rownorm, n_prompt_rows=n_p, rows_per_seq=seq)
    p_conv, p_rec, p_k, p_v, s_conv, s_rec, s_k, s_v = [], [], [], [], [], [], [], []
    modp, modr = modulation(0)
    (hm,) = rn(x, norm_g, 0, modp, modr, mod=(MOD_SC_M, MOD_SH_M), name="norm_mix")
    y = None
    for l in range(depth):
        j = l // 2
        if l % 2 == 0:
            mix, cv_p, st_p, cv_s, st_s = _deltanet(hm, j, w, state_dn_conv, state_dn_rec, dims)
            p_conv.append(cv_p), p_rec.append(st_p), s_conv.append(cv_s), s_rec.append(st_s)
        else:
            mix, kk_p, vv_p, kk_s, vv_s = _swa(hm, j, w, cache_swa_k, cache_swa_v, dims, past_len)
            p_k.append(kk_p), p_v.append(vv_p), s_k.append(kk_s), s_v.append(vv_s)
        x, hf, hf_slab = rn(x, norm_g, 2 * l + 1, modp, modr, resid=(mix, MOD_GT_M), mod=(MOD_SC_F, MOD_SH_F),
                            out_x=True, out_slab=True, name="resid_norm_ffn")
        moe = _moe(hf, hf_slab, l, w, n_valid_rows=n_p + bs)
        if l + 1 < depth:
            gt_mods = (modp, modr)
            modp, modr = modulation(l + 1)
            x, hm = _resid_then_norm(x, moe, gt_mods, norm_g, 2 * (l + 1), (modp, modr), n_p, seq)
        else:
            (y,) = rn(x, final_g, 0, modp, modr, resid=(moe, MOD_GT_F), h_dtype=F32, name="final_norm")

    y_prompt = y[:n_p].reshape(batch, seq, d)
    y_sample = y[n_p:n_p + bs].reshape(bs, 1, d)
    return (y_prompt, y_sample, jnp.stack(p_conv), jnp.stack(p_rec), jnp.stack(p_k), jnp.stack(p_v),
            jnp.stack(s_conv), jnp.stack(s_rec), jnp.stack(s_k), jnp.stack(s_v))


def _resid_then_norm(x, y, gt_mods, norm_g, g_row, next_mods, n_p, seq):
    modp, modr = gt_mods
    nmodp, nmodr = next_mods
    modp2 = jnp.concatenate([modp[:, MOD_GT_F:MOD_GT_F + 1], nmodp[:, MOD_SC_M:MOD_SC_M + 1], nmodp[:, MOD_SH_M:MOD_SH_M + 1]], axis=1)
    modr2 = jnp.stack([modr[MOD_GT_F], nmodr[MOD_SC_M], nmodr[MOD_SH_M]], axis=0)
    return _rownorm(x, norm_g, g_row, modp2, modr2, n_prompt_rows=n_p, rows_per_seq=seq, resid=(y, 0), mod=(1, 2),
                    out_x=True, name="resid_norm_mix")
```

```python
import functools

import jax
import jax.numpy as jnp
from jax import lax
from jax.experimental import pallas as pl
from jax.experimental.pallas import tpu as pltpu

F32 = jnp.float32
BF16 = jnp.bfloat16
I32 = jnp.int32

NORM_EPS = 1e-6
ROPE_THETA = 10000.0
ROUTED_SCALE = 2.5
MOE_GROUPS = 8
MOE_TOPK_GROUPS = 4
MOE_TOP_K = 8
SWA_WINDOW = 128
DN_CHUNK = 64
DN_SOLVE_BLOCK = 16

LANES = 128
SUBLANES = 8
VMEM_LIMIT = 60 * 1024 * 1024

ROW_TILE = 256
NORM_TILE = 128
MM_LHS_TILE_ELEMS = 1408 * 4096
MOE_TILE = 256
COMBINE_TILE = 64
RANK_BITS = 20
TRANSPOSE_ROWS = 64
DMA_UNROLL = 8
PAST_LEN = 16384
DN_HEADS_PER_STEP = 16
DN_HEADS_PER_MATMUL = 4


def _cparams(sem, vmem=None, bounds_checks=True):
    return pltpu.CompilerParams(dimension_semantics=sem, vmem_limit_bytes=vmem, disable_bounds_checks=not bounds_checks)


def _round_up(x, m):
    return (x + m - 1) // m * m


def _pick_tile(n, pref, mult=SUBLANES):
    if n <= pref:
        return n
    for t in range(pref - pref % mult, 0, -mult):
        if n % t == 0:
            return t
    return n


def _silu(x):
    return x * jax.nn.sigmoid(x)


def _bdot(a, b):
    return jnp.dot(a.astype(BF16), b.astype(BF16), preferred_element_type=F32)


def _bdot_nt(a, b):
    return lax.dot_general(a.astype(BF16), b.astype(BF16), (((1,), (1,)), ((), ())), preferred_element_type=F32)


def _bdot_tn(a, b):
    return lax.dot_general(a.astype(BF16), b.astype(BF16), (((0,), (0,)), ((), ())), preferred_element_type=F32)


def _split3(x):
    h1 = x.astype(BF16)
    r1 = x - h1.astype(F32)
    h2 = r1.astype(BF16)
    h3 = (r1 - h2.astype(F32)).astype(BF16)
    return h1, h2, h3


def _mm_kernel(a_ref, w_ref, *rest, lhs_silu, has_bias):
    if has_bias:
        b_ref, o_ref, wc_ref = rest
    else:
        o_ref, wc_ref = rest

    @pl.when(pl.program_id(1) == 0)
    def _():
        wc_ref[...] = w_ref[...].astype(BF16)

    a = a_ref[...]
    if lhs_silu:
        a = _silu(a.astype(F32))
    acc = jnp.dot(a.astype(BF16), wc_ref[...], preferred_element_type=F32)
    if has_bias:
        acc = acc + b_ref[...]
    o_ref[...] = acc.astype(o_ref.dtype)


def _mm(a, w, l, bias=None, *, cols=None, out_dtype=F32, lhs_silu=False, name):
    m, k = a.shape
    c0, c1 = cols if cols is not None else (0, w.shape[2])
    n = c1 - c0
    tm = _pick_tile(m, MM_LHS_TILE_ELEMS // k, 16)
    tn = _pick_tile(n, 512 if k <= 4096 else 256, LANES)
    assert c0 % tn == 0 and n % tn == 0
    jb = c0 // tn
    grid = (n // tn, m // tm)
    in_specs = [
        pl.BlockSpec((tm, k), lambda j, i: (i, 0)),
        pl.BlockSpec((None, k, tn), lambda j, i: (l, 0, jb + j)),
    ]
    args = [a, w]
    if bias is not None:
        in_specs.append(pl.BlockSpec((None, 1, tn), lambda j, i: (l, 0, jb + j)))
        args.append(bias.reshape(bias.shape[0], 1, w.shape[2]))
    return pl.pallas_call(
        functools.partial(_mm_kernel, lhs_silu=lhs_silu, has_bias=bias is not None),
        grid=grid,
        in_specs=in_specs,
        out_specs=pl.BlockSpec((tm, tn), lambda j, i: (i, j)),
        out_shape=jax.ShapeDtypeStruct((m, n), out_dtype),
        scratch_shapes=[pltpu.VMEM((k, tn), BF16)],
        compiler_params=_cparams(("arbitrary", "arbitrary"), VMEM_LIMIT),
        name=name,
    )(*args)


def _glu_kernel(a_ref, wg_ref, wu_ref, o_ref, wgc_ref, wuc_ref):
    @pl.when(pl.program_id(1) == 0)
    def _():
        wgc_ref[...] = wg_ref[...].astype(BF16)
        wuc_ref[...] = wu_ref[...].astype(BF16)

    a = a_ref[...]
    g = jnp.dot(a, wgc_ref[...], preferred_element_type=F32)
    u = jnp.dot(a, wuc_ref[...], preferred_element_type=F32)
    o_ref[...] = (_silu(g) * u).astype(o_ref.dtype)


def _glu(a, w, l, *, name):
    m, k = a.shape
    h = w.shape[2] // 2
    tm = _pick_tile(m, 768, 16)
    tn = _pick_tile(h, 256, LANES)
    nj = h // tn
    return pl.pallas_call(
        _glu_kernel,
        grid=(nj, m // tm),
        in_specs=[
            pl.BlockSpec((tm, k), lambda j, i: (i, 0)),
            pl.BlockSpec((None, k, tn), lambda j, i: (l, 0, j)),
            pl.BlockSpec((None, k, tn), lambda j, i: (l, 0, nj + j)),
        ],
        out_specs=pl.BlockSpec((tm, tn), lambda j, i: (i, j)),
        out_shape=jax.ShapeDtypeStruct((m, h), BF16),
        scratch_shapes=[pltpu.VMEM((k, tn), BF16), pltpu.VMEM((k, tn), BF16)],
        compiler_params=_cparams(("arbitrary", "arbitrary"), VMEM_LIMIT),
        name=name,
    )(a, w, w)


MOD_SH_M, MOD_SC_M, MOD_GT_M, MOD_SH_F, MOD_SC_F, MOD_GT_F = range(6)


def _rownorm_kernel(*refs, n_prompt_tiles, has_resid, has_mod, out_x, out_slab, d):
    refs = list(refs)
    x_ref = refs.pop(0)
    if has_resid:
        y_ref, gtb_ref, gtr_ref = refs.pop(0), refs.pop(0), refs.pop(0)
    g_ref = refs.pop(0)
    if has_mod:
        scb_ref, shb_ref, scr_ref, shr_ref = refs.pop(0), refs.pop(0), refs.pop(0), refs.pop(0)
    outs = refs
    is_prompt = pl.program_id(0) < n_prompt_tiles

    def body(prompt):
        x = x_ref[...]
        if has_resid:
            gt = gtb_ref[...] if prompt else gtr_ref[...]
            x = x + gt * y_ref[...]
        o = list(outs)
        if out_x:
            o.pop(0)[...] = x
        h = x * lax.rsqrt(jnp.mean(x * x, axis=-1, keepdims=True) + NORM_EPS) * g_ref[...]
        if has_mod:
            sc = scb_ref[...] if prompt else scr_ref[...]
            sh = shb_ref[...] if prompt else shr_ref[...]
            h = h * (1.0 + sc) + sh
        h_ref = o.pop(0)
        h_ref[...] = h.astype(h_ref.dtype)
        if out_slab:
            slab_ref = o.pop(0)
            sub = TRANSPOSE_ROWS
            for t0 in range(0, x.shape[0], sub):
                cols = [h[t0:t0 + sub, s * LANES:(s + 1) * LANES] for s in range(d // LANES)]
                slab_ref[t0:t0 + sub] = jnp.swapaxes(jnp.stack(cols, axis=0), 0, 1)

    @pl.when(is_prompt)
    def _():
        body(True)

    @pl.when(jnp.logical_not(is_prompt))
    def _():
        body(False)


def _rownorm(x, g, l, modp, modr, *, n_prompt_rows, rows_per_seq, resid=None, mod=None, out_x=False,
             h_dtype=BF16, out_slab=False, name):
    rows, d = x.shape
    tr = NORM_TILE
    n_ptiles = n_prompt_rows // tr
    tiles_per_seq = rows_per_seq // tr
    nb = modp.shape[0]

    def bspec(c):
        return pl.BlockSpec((None, None, 1, d), lambda i: (jnp.minimum(i // tiles_per_seq, nb - 1), c, 0, 0))

    def rspec(c):
        return pl.BlockSpec((None, tr, d), lambda i: (c, jnp.maximum(i - n_ptiles, 0), 0))

    row_spec = pl.BlockSpec((tr, d), lambda i: (i, 0))
    in_specs, args = [row_spec], [x]
    if resid is not None:
        y, gt_c = resid
        in_specs += [row_spec, bspec(gt_c), rspec(gt_c)]
        args += [y, modp, modr]
    in_specs.append(pl.BlockSpec((None, 1, d), lambda i: (l, 0, 0)))
    args.append(g.reshape(g.shape[0], 1, d))
    if mod is not None:
        sc_c, sh_c = mod
        in_specs += [bspec(sc_c), bspec(sh_c), rspec(sc_c), rspec(sh_c)]
        args += [modp, modp, modr, modr]
    out_specs, out_shape = [], []
    if out_x:
        out_specs.append(row_spec)
        out_shape.append(jax.ShapeDtypeStruct((rows, d), F32))
    out_specs.append(row_spec)
    out_shape.append(jax.ShapeDtypeStruct((rows, d), h_dtype))
    if out_slab:
        out_specs.append(pl.BlockSpec((tr, d // LANES, LANES), lambda i: (i, 0, 0)))
        out_shape.append(jax.ShapeDtypeStruct((rows, d // LANES, LANES), F32))
    return pl.pallas_call(
        functools.partial(_rownorm_kernel, n_prompt_tiles=n_ptiles, has_resid=resid is not None,
                          has_mod=mod is not None, out_x=out_x, out_slab=out_slab, d=d),
        grid=(rows // tr,),
        in_specs=in_specs,
        out_specs=out_specs,
        out_shape=out_shape,
        compiler_params=_cparams(("arbitrary",), VMEM_LIMIT),
        name=name,
    )(*args)


def _l2norm_groups(y, width):
    parts = []
    for s in range(y.shape[1] // width):
        seg = y[:, s * width:(s + 1) * width]
        parts.append(seg * lax.rsqrt(jnp.sum(seg * seg, axis=-1, keepdims=True) + NORM_EPS))
    return parts


def _dn_conv_kernel(x_ref, w_ref, o_ref, xs_ref, *, tiles_per_seq, n_norm_tiles, dk):
    tr = x_ref.shape[0]
    j = pl.program_id(0)
    first = (pl.program_id(1) % tiles_per_seq) == 0

    @pl.when(first)
    def _():
        xs_ref[0:SUBLANES, :] = jnp.zeros((SUBLANES, xs_ref.shape[1]), F32)

    @pl.when(jnp.logical_not(first))
    def _():
        xs_ref[0:SUBLANES, :] = xs_ref[tr:tr + SUBLANES, :]

    xs_ref[SUBLANES:SUBLANES + tr, :] = x_ref[...]
    w = w_ref[...]
    nw = w.shape[0]
    acc = xs_ref[SUBLANES:SUBLANES + tr, :] * w[nw - 1:nw, :]
    for t in range(1, nw):
        acc = acc + xs_ref[SUBLANES - t:SUBLANES - t + tr, :] * w[nw - 1 - t:nw - t, :]
    y = _silu(acc)

    @pl.when(j < n_norm_tiles)
    def _():
        for s, seg in enumerate(_l2norm_groups(y, dk)):
            o_ref[:, s * dk:(s + 1) * dk] = seg

    @pl.when(j >= n_norm_tiles)
    def _():
        o_ref[...] = y


def _dn_conv_prompt(p, conv_w, l, *, n_rows, rows_per_seq, conv_ch, qk_dim, dk):
    tr = _pick_tile(rows_per_seq, 2 * ROW_TILE)
    tc = _pick_tile(qk_dim, 1024, dk)
    return pl.pallas_call(
        functools.partial(_dn_conv_kernel, tiles_per_seq=rows_per_seq // tr, n_norm_tiles=2 * qk_dim // tc, dk=dk),
        grid=(conv_ch // tc, n_rows // tr),
        in_specs=[
            pl.BlockSpec((tr, tc), lambda j, i: (i, j)),
            pl.BlockSpec((None, conv_w.shape[1], tc), lambda j, i: (l, 0, j)),
        ],
        out_specs=pl.BlockSpec((tr, tc), lambda j, i: (i, j)),
        out_shape=jax.ShapeDtypeStruct((n_rows, conv_ch), F32),
        scratch_shapes=[pltpu.VMEM((tr + SUBLANES, tc), F32)],
        compiler_params=_cparams(("arbitrary", "arbitrary"), VMEM_LIMIT),
        name="dn_conv_prompt",
    )(p, conv_w)


def _dn_conv_step_kernel(x_ref, prev_ref, w_ref, o_ref, *, n_norm_tiles, dk):
    w = w_ref[...]
    nw = w.shape[0]
    acc = x_ref[...] * w[nw - 1:nw, :]
    for t in range(nw - 1):
        acc = acc + prev_ref[t] * w[t:t + 1, :]
    y = _silu(acc)

    @pl.when(pl.program_id(0) < n_norm_tiles)
    def _():
        for s, seg in enumerate(_l2norm_groups(y, dk)):
            o_ref[:, s * dk:(s + 1) * dk] = seg

    @pl.when(pl.program_id(0) >= n_norm_tiles)
    def _():
        o_ref[...] = y


def _dn_conv_step(p, prev, conv_w, l, *, row0, rows, conv_ch, qk_dim, dk):
    tc = _pick_tile(qk_dim, 512, dk)
    rb = row0 // rows
    return pl.pallas_call(
        functools.partial(_dn_conv_step_kernel, n_norm_tiles=2 * qk_dim // tc, dk=dk),
        grid=(conv_ch // tc,),
        in_specs=[
            pl.BlockSpec((rows, tc), lambda j: (rb, j)),
            pl.BlockSpec((prev.shape[0], rows, tc), lambda j: (0, 0, j)),
            pl.BlockSpec((None, conv_w.shape[1], tc), lambda j: (l, 0, j)),
        ],
        out_specs=pl.BlockSpec((rows, tc), lambda j: (0, j)),
        out_shape=jax.ShapeDtypeStruct((rows, conv_ch), F32),
        compiler_params=_cparams(("arbitrary",), VMEM_LIMIT),
        name="dn_conv_step",
    )(p, prev, conv_w)


def _dn_gate_kernel(ba_ref, alog_ref, dtb_ref, beta_ref, g_ref, gc_ref, *, hv, chunk):
    ba = ba_ref[...]
    beta_ref[...] = jax.nn.sigmoid(ba[:, :hv])
    x = ba[:, hv:] + dtb_ref[...]
    g = -jnp.exp(alog_ref[...]) * (jnp.maximum(x, 0.0) + jnp.log1p(jnp.exp(-jnp.abs(x))))
    g_ref[...] = g
    tril = (lax.broadcasted_iota(I32, (chunk, chunk), 0) >= lax.broadcasted_iota(I32, (chunk, chunk), 1)).astype(BF16)
    for c in range(ba.shape[0] // chunk):
        h1, h2, h3 = _split3(g[c * chunk:(c + 1) * chunk, :])
        gc = (jnp.dot(tril, h1, preferred_element_type=F32) + jnp.dot(tril, h2, preferred_element_type=F32)
              + jnp.dot(tril, h3, preferred_element_type=F32))
        gc_ref[c * chunk:(c + 1) * chunk, :] = gc


def _dn_gates(ba, a_log, dt_bias, l, *, hv):
    rows = ba.shape[0]
    tr = ROW_TILE
    spec = pl.BlockSpec((tr, hv), lambda i: (i, 0))
    pspec = pl.BlockSpec((None, 1, hv), lambda i: (l, 0, 0))
    shp = jax.ShapeDtypeStruct((rows, hv), F32)
    return pl.pallas_call(
        functools.partial(_dn_gate_kernel, hv=hv, chunk=DN_CHUNK),
        grid=(rows // tr,),
        in_specs=[pl.BlockSpec((tr, 2 * hv), lambda i: (i, 0)), pspec, pspec],
        out_specs=[spec, spec, spec],
        out_shape=[shp, shp, shp],
        compiler_params=_cparams(("arbitrary",)),
        name="dn_gates",
    )(ba, a_log.reshape(a_log.shape[0], 1, hv), dt_bias.reshape(dt_bias.shape[0], 1, hv))


def _unit_lower_solve(lmat, rhs, eye_f, blk, n_outer):
    n = lmat.shape[0]
    ld = jnp.where(blk, lmat, 0.0)
    lo = lmat - ld
    p = -ld
    dinv = eye_f + p
    span = 2
    while span < DN_SOLVE_BLOCK:
        if span == 2:
            p = _bdot(p, p)
        if 2 * span < DN_SOLVE_BLOCK:
            r = _bdot(p, jnp.concatenate([dinv, p], axis=1))
            dinv = dinv + r[:, :n]
            p = r[:, n:]
        else:
            dinv = dinv + _bdot(p, dinv)
        span *= 2
    r = _bdot(dinv, jnp.concatenate([lo, rhs], axis=1))
    nmat = -r[:, :n]
    y0 = r[:, n:]
    y = y0
    for _ in range(n_outer - 1):
        y = y0 + _bdot(nmat, y)
    return y


def _gated_rmsnorm(o, z, g):
    y = o * lax.rsqrt(jnp.mean(o * o, axis=-1, keepdims=True) + NORM_EPS) * g
    return y * _silu(z)


def _dn_chunk_kernel(q_ref, k_ref, v_ref, z_ref, beta_ref, gc_ref, gl_ref, ng_ref, og_ref, sout_ref, s_ref,
                     *, heads, rep, dk, dv):
    c = q_ref.shape[0]
    ci = pl.program_id(2)

    @pl.when(ci == 0)
    def _():
        s_ref[...] = jnp.zeros(s_ref.shape, F32)

    pack = min(DN_HEADS_PER_MATMUL, heads)
    n = pack * c
    row = lax.broadcasted_iota(I32, (n, n), 0)
    col = lax.broadcasted_iota(I32, (n, n), 1)
    eye = row == col
    eye_f = eye.astype(F32)
    same_head = (row // c) == (col // c)
    causal = same_head & (row >= col)
    strict = same_head & (row > col)
    blk = (row // DN_SOLVE_BLOCK) == (col // DN_SOLVE_BLOCK)
    own_state = (lax.broadcasted_iota(I32, (n, pack * dv), 0) // c) == (lax.broadcasted_iota(I32, (n, pack * dv), 1) // dv)
    ng = ng_ref[...]
    beta = beta_ref[...]
    gc = gc_ref[...]
    gl = gl_ref[...]

    def rows(fn):
        return jnp.concatenate([fn(s) for s in hs], axis=0)

    for s0 in range(0, heads, pack):
        hs = range(s0, s0 + pack)
        qst = rows(lambda s: q_ref[:, (s // rep) * dk:(s // rep + 1) * dk]) * (dk ** -0.5)
        kst = rows(lambda s: k_ref[:, (s // rep) * dk:(s // rep + 1) * dk])
        vst = rows(lambda s: v_ref[:, s * dv:(s + 1) * dv])
        zst = rows(lambda s: z_ref[:, s * dv:(s + 1) * dv])
        bcol = rows(lambda s: beta[:, s:s + 1])
        gcol = rows(lambda s: gc[:, s:s + 1])
        glcol = rows(lambda s: jnp.broadcast_to(gl[:, s:s + 1], (c, 1)))
        grow = jnp.sum(jnp.where(eye, gcol, 0.0), axis=0, keepdims=True)
        decay = jnp.exp(jnp.where(causal, gcol - grow, -jnp.inf))
        egc = jnp.exp(gcol)
        kb = kst * bcol
        kq = _bdot_nt(jnp.concatenate([kb, qst], axis=0), kst)
        lmat = jnp.where(strict, kq[:n] * decay, 0.0)
        attn = kq[n:] * decay
        sol = _unit_lower_solve(lmat, jnp.concatenate([vst * bcol, kb * egc], axis=1), eye_f, blk, c // DN_SOLVE_BLOCK)
        u = sol[:, :dv]
        w = sol[:, dv:]
        st = s_ref[:, s0 * dv:(s0 + pack) * dv]
        wq = _bdot(jnp.concatenate([w, qst * egc], axis=0), st)

        def own_cols(x):
            return jnp.concatenate([x[h * c:(h + 1) * c, h * dv:(h + 1) * dv] for h in range(pack)], axis=0)

        v_new = u - own_cols(wq[:n])
        o = own_cols(wq[n:]) + _bdot(attn, v_new)
        kd = kst * jnp.exp(glcol - gcol)
        v_bd = jnp.where(own_state, jnp.concatenate([v_new] * pack, axis=1), 0.0)
        eg = jnp.concatenate([jnp.broadcast_to(jnp.exp(gl[:, s:s + 1]), (1, dv)) for s in hs], axis=1)
        s_ref[:, s0 * dv:(s0 + pack) * dv] = st * eg + _bdot_tn(kd, v_bd)
        og = _gated_rmsnorm(o, zst, ng).astype(og_ref.dtype)
        for h, s in enumerate(hs):
            og_ref[:, s * dv:(s + 1) * dv] = og[h * c:(h + 1) * c]

    @pl.when(ci == pl.num_programs(2) - 1)
    def _():
        for s in range(heads):
            sout_ref[s] = s_ref[:, s * dv:(s + 1) * dv]


def _dn_chunks(qkv, p, beta3, gc3, gl4, norm_g, l, *, batch, seq, hk, hv, dk, dv, conv_ch):
    c = DN_CHUNK
    hb = min(DN_HEADS_PER_STEP, hv)
    rep = hv // hk
    nc = seq // c
    qw = (hb // rep) * dk
    vw = hb * dv
    qk_dim = hk * dk

    def rowmap(off):
        return lambda b, g, i: (b * nc + i, off + g)

    return pl.pallas_call(
        functools.partial(_dn_chunk_kernel, heads=hb, rep=rep, dk=dk, dv=dv),
        grid=(batch, hv // hb, nc),
        in_specs=[
            pl.BlockSpec((c, qw), rowmap(0)),
            pl.BlockSpec((c, qw), rowmap(qk_dim // qw)),
            pl.BlockSpec((c, vw), rowmap(2 * qk_dim // vw)),
            pl.BlockSpec((c, vw), rowmap(conv_ch // vw)),
            pl.BlockSpec((None, c, hb), lambda b, g, i: (g, b * nc + i, 0)),
            pl.BlockSpec((None, c, hb), lambda b, g, i: (g, b * nc + i, 0)),
            pl.BlockSpec((None, None, 1, hb), lambda b, g, i: (g, b * nc + i, 0, 0)),
            pl.BlockSpec((None, 1, dv), lambda b, g, i: (l, 0, 0)),
        ],
        out_specs=[
            pl.BlockSpec((c, vw), lambda b, g, i: (b * nc + i, g)),
            pl.BlockSpec((None, hb, dk, dv), lambda b, g, i: (b, g, 0, 0)),
        ],
        out_shape=[
            jax.ShapeDtypeStruct((batch * seq, hv * dv), BF16),
            jax.ShapeDtypeStruct((batch, hv, dk, dv), F32),
        ],
        scratch_shapes=[pltpu.VMEM((dk, hb * dv), F32)],
        compiler_params=_cparams(("arbitrary", "arbitrary", "arbitrary"), VMEM_LIMIT),
        name="dn_chunks",
    )(qkv, qkv, qkv, p, beta3, gc3, gl4, norm_g.reshape(norm_g.shape[0], 1, dv))


def _dn_step_kernel(q_ref, k_ref, v_ref, z_ref, beta_ref, g_ref, ng_ref, sin_ref, og_ref, sout_ref,
                    *, heads, rep, dk, dv):
    eye = lax.broadcasted_iota(I32, (dk, dk), 0) == lax.broadcasted_iota(I32, (dk, dk), 1)
    ng = ng_ref[...]
    beta = beta_ref[...]
    g = g_ref[...]

    def column(r):
        return jnp.sum(jnp.where(eye, r, 0.0), axis=1, keepdims=True)

    for s in range(heads):
        hq = s // rep
        q_col = column(q_ref[:, hq * dk:(hq + 1) * dk])
        k_col = column(k_ref[:, hq * dk:(hq + 1) * dk])
        v_row = v_ref[:, s * dv:(s + 1) * dv]
        a = jnp.exp(g[:, s:s + 1])
        st = sin_ref[s]
        ks = jnp.sum(st * k_col, axis=0, keepdims=True)
        v_new = beta[:, s:s + 1] * (v_row - a * ks)
        s_new = a * st + k_col * v_new
        sout_ref[s] = s_new
        o = jnp.sum(s_new * q_col, axis=0, keepdims=True) * (dk ** -0.5)
        og_ref[:, s * dv:(s + 1) * dv] = _gated_rmsnorm(o, z_ref[:, s * dv:(s + 1) * dv], ng)


def _dn_step(qkv3, z3, beta4, g4, norm_g, l, s_in, *, hk, hv, dk, dv):
    bs = s_in.shape[0]
    hb = min(DN_HEADS_PER_STEP, hv)
    rep = hv // hk
    qw = (hb // rep) * dk
    vw = hb * dv
    qk_dim = hk * dk
    return pl.pallas_call(
        functools.partial(_dn_step_kernel, heads=hb, rep=rep, dk=dk, dv=dv),
        grid=(bs, hv // hb),
        in_specs=[
            pl.BlockSpec((None, 1, qw), lambda b, g: (b, 0, g)),
            pl.BlockSpec((None, 1, qw), lambda b, g: (b, 0, qk_dim // qw + g)),
            pl.BlockSpec((None, 1, vw), lambda b, g: (b, 0, 2 * qk_dim // vw + g)),
            pl.BlockSpec((None, 1, vw), lambda b, g: (b, 0, g)),
            pl.BlockSpec((None, None, 1, hb), lambda b, g: (g, b, 0, 0)),
            pl.BlockSpec((None, None, 1, hb), lambda b, g: (g, b, 0, 0)),
            pl.BlockSpec((None, 1, dv), lambda b, g: (l, 0, 0)),
            pl.BlockSpec((None, hb, dk, dv), lambda b, g: (b, g, 0, 0)),
        ],
        out_specs=[
            pl.BlockSpec((None, 1, vw), lambda b, g: (b, 0, g)),
            pl.BlockSpec((None, hb, dk, dv), lambda b, g: (b, g, 0, 0)),
        ],
        out_shape=[
            jax.ShapeDtypeStruct((bs, 1, hv * dv), F32),
            jax.ShapeDtypeStruct(s_in.shape, F32),
        ],
        compiler_params=_cparams(("arbitrary", "arbitrary"), VMEM_LIMIT),
        name="dn_step",
    )(qkv3, qkv3, qkv3, z3, beta4, g4, norm_g.reshape(norm_g.shape[0], 1, dv), s_in)


def _head_groups(x, hb):
    rows, h = x.shape
    return x.reshape(rows, h // hb, hb).transpose(1, 0, 2)


def _deltanet(hm, j, w, st_conv, st_rec, dims):
    n_p, seq, batch, bs, np_rows = dims["n_p"], dims["seq"], dims["batch"], dims["bs"], dims["np_rows"]
    dk, dv = st_rec.shape[3], st_rec.shape[4]
    hv = st_rec.shape[2]
    conv_ch = st_conv.shape[3]
    v_dim = hv * dv
    qk_dim = (conv_ch - v_dim) // 2
    hk = qk_dim // dk
    hb = min(DN_HEADS_PER_STEP, hv)

    dn_in = w["dn_w_in"].shape[2]
    p = _mm(hm, w["dn_w_in"], j, cols=(0, conv_ch + v_dim), name="dn_in_proj")
    ba = _mm(hm, w["dn_w_in"], j, cols=(conv_ch + v_dim, dn_in), name="dn_in_proj_gates")
    beta, g, gc = _dn_gates(ba, w["dn_a_log"], w["dn_dt_bias"], j, hv=hv)

    qkv = _dn_conv_prompt(p, w["dn_conv_w"], j, n_rows=n_p, rows_per_seq=seq, conv_ch=conv_ch, qk_dim=qk_dim, dk=dk)
    gl = gc[DN_CHUNK - 1:n_p:DN_CHUNK]
    og_p, rec_p = _dn_chunks(
        qkv, p, _head_groups(beta[:n_p], hb), _head_groups(gc[:n_p], hb),
        _head_groups(gl, hb)[:, :, None, :], w["dn_norm_g"], j,
        batch=batch, seq=seq, hk=hk, hv=hv, dk=dk, dv=dv, conv_ch=conv_ch)
    nw = w["dn_conv_w"].shape[1]
    conv_p = jnp.stack([lax.slice(p, ((b + 1) * seq - (nw - 1), 0), ((b + 1) * seq, conv_ch)) for b in range(batch)])

    prev = jnp.transpose(st_conv[j], (1, 0, 2))
    qkv_s = _dn_conv_step(p, prev, w["dn_conv_w"], j, row0=n_p, rows=bs, conv_ch=conv_ch, qk_dim=qk_dim, dk=dk)
    z_s = p[n_p:n_p + bs, conv_ch:conv_ch + v_dim]
    og_s, rec_s = _dn_step(
        qkv_s[:, None, :], z_s[:, None, :], _head_groups(beta[n_p:n_p + bs], hb)[:, :, None, :],
        _head_groups(g[n_p:n_p + bs], hb)[:, :, None, :], w["dn_norm_g"], j, st_rec[j],
        hk=hk, hv=hv, dk=dk, dv=dv)
    conv_s = jnp.concatenate([st_conv[j][:, 1:], p[n_p:n_p + bs, None, :conv_ch]], axis=1)

    og = jnp.concatenate(
        [og_p, og_s.reshape(bs, v_dim).astype(BF16), jnp.zeros((np_rows - n_p - bs, v_dim), BF16)], axis=0)
    mix = _mm(og, w["dn_w_out"], j, name="dn_out_proj")
    return mix, conv_p, rec_p, conv_s, rec_s


def _rope_kernel(x_ref, cos_ref, sin_ref, o_ref, *, hd):
    cos = cos_ref[...]
    sin = sin_ref[...]
    lane = lax.broadcasted_iota(I32, cos.shape, 1)
    first_half = (lane % hd) < hd // 2
    for s in range(x_ref.shape[1] // LANES):
        x = x_ref[:, s * LANES:(s + 1) * LANES]
        rot = jnp.where(first_half, pltpu.roll(x, LANES - hd // 2, axis=1), pltpu.roll(x, hd // 2, axis=1))
        o_ref[:, s * LANES:(s + 1) * LANES] = x * cos + rot * sin


def _rope(p2, cos, sin, *, width, hd):
    rows = p2.shape[0]
    tr = ROW_TILE
    tc = _pick_tile(width, 512, LANES)
    return pl.pallas_call(
        functools.partial(_rope_kernel, hd=hd),
        grid=(rows // tr, width // tc),
        in_specs=[
            pl.BlockSpec((tr, tc), lambda i, j: (i, j)),
            pl.BlockSpec((tr, LANES), lambda i, j: (i, 0)),
            pl.BlockSpec((tr, LANES), lambda i, j: (i, 0)),
        ],
        out_specs=pl.BlockSpec((tr, tc), lambda i, j: (i, j)),
        out_shape=jax.ShapeDtypeStruct((rows, width), F32),
        compiler_params=_cparams(("arbitrary", "arbitrary")),
        name="swa_rope",
    )(p2, cos, sin)


def _swa_prompt_kernel(q_ref, kp_ref, ko_ref, vp_ref, vo_ref, sink_ref, o_ref, *, kv, group, hd, window):
    blk = q_ref.shape[0]
    i = pl.program_id(1)
    rows = group * blk
    r = lax.broadcasted_iota(I32, (rows, 2 * blk), 0) % blk
    c = lax.broadcasted_iota(I32, (rows, 2 * blk), 1)
    mask = (c - blk <= r) & (c - blk > r - window) & ((i > 0) | (c >= blk))
    sinks = sink_ref[...]
    scale = hd ** -0.5
    for n in range(kv):
        hs = range(n * group, (n + 1) * group)
        kk = jnp.concatenate([kp_ref[:, n * hd:(n + 1) * hd], ko_ref[:, n * hd:(n + 1) * hd]], axis=0).astype(BF16)
        vv = jnp.concatenate([vp_ref[:, n * hd:(n + 1) * hd], vo_ref[:, n * hd:(n + 1) * hd]], axis=0).astype(BF16)
        qs = jnp.concatenate([q_ref[:, h * hd:(h + 1) * hd] for h in hs], axis=0)
        sink = jnp.concatenate([jnp.broadcast_to(sinks[:, h:h + 1], (blk, 1)) for h in hs], axis=0)
        s = jnp.where(mask, _bdot_nt(qs, kk) * scale, -jnp.inf)
        m = jnp.maximum(jnp.max(s, axis=-1, keepdims=True), sink)
        e = jnp.exp(s - m)
        pn = e / (jnp.sum(e, axis=-1, keepdims=True) + jnp.exp(sink - m))
        o = jnp.dot(pn.astype(BF16), vv, preferred_element_type=F32).astype(o_ref.dtype)
        for gq, h in enumerate(hs):
            o_ref[:, h * hd:(h + 1) * hd] = o[gq * blk:(gq + 1) * blk]


def _swa_prompt(rot, p2, sinks, l, *, batch, seq, heads, kv, hd):
    blk = SWA_WINDOW
    nb = seq // blk
    qw = heads * hd
    kw = kv * hd
    kcol = qw // kw

    def own(off):
        return lambda b, i: (b * nb + i, off)

    def prev(off):
        return lambda b, i: (b * nb + jnp.maximum(i - 1, 0), off)

    return pl.pallas_call(
        functools.partial(_swa_prompt_kernel, kv=kv, group=heads // kv, hd=hd, window=SWA_WINDOW),
        grid=(batch, nb),
        in_specs=[
            pl.BlockSpec((blk, qw), own(0)),
            pl.BlockSpec((blk, kw), prev(kcol)),
            pl.BlockSpec((blk, kw), own(kcol)),
            pl.BlockSpec((blk, kw), prev(kcol + 1)),
            pl.BlockSpec((blk, kw), own(kcol + 1)),
            pl.BlockSpec((None, 1, heads), lambda b, i: (l, 0, 0)),
        ],
        out_specs=pl.BlockSpec((blk, qw), own(0)),
        out_shape=jax.ShapeDtypeStruct((batch * seq, qw), BF16),
        compiler_params=_cparams(("arbitrary", "arbitrary"), VMEM_LIMIT),
        name="swa_prompt_attn",
    )(rot, rot, rot, p2, p2, sinks.reshape(sinks.shape[0], 1, heads))


def _swa_step_kernel(q_ref, kn_ref, vn_ref, ck_ref, cv_ref, sink_ref, o_ref, *, kv, group, hd, first_valid):
    buf = ck_ref.shape[0]
    valid = lax.broadcasted_iota(I32, (group, buf), 1) >= first_valid
    scale = hd ** -0.5
    for n in range(kv):
        qg = q_ref[n * group:(n + 1) * group, :]
        kc = ck_ref[:, n * hd:(n + 1) * hd]
        vc = cv_ref[:, n * hd:(n + 1) * hd]
        kn = kn_ref[n:n + 1, :]
        vn = vn_ref[n:n + 1, :]
        sink = sink_ref[n * group:(n + 1) * group, :]
        s = jnp.where(valid, _bdot_nt(qg, kc) * scale, -jnp.inf)
        s_new = jnp.sum(qg.astype(BF16).astype(F32) * kn.astype(BF16).astype(F32), axis=-1, keepdims=True) * scale
        m = jnp.maximum(jnp.maximum(jnp.max(s, axis=-1, keepdims=True), s_new), sink)
        e = jnp.exp(s - m)
        e_new = jnp.exp(s_new - m)
        den = jnp.sum(e, axis=-1, keepdims=True) + e_new + jnp.exp(sink - m)
        o = _bdot(e / den, vc) + (e_new / den) * vn
        o_ref[n * group:(n + 1) * group, :] = o


def _swa_step(q3, kn3, vn3, ck, cv, sinks, l, *, heads, kv, hd):
    bs, buf = ck.shape[0], ck.shape[1]
    kw = kv * hd
    first_valid = max(buf - SWA_WINDOW + 1, 0)
    return pl.pallas_call(
        functools.partial(_swa_step_kernel, kv=kv, group=heads // kv, hd=hd, first_valid=first_valid),
        grid=(bs,),
        in_specs=[
            pl.BlockSpec((None, heads, hd), lambda b: (b, 0, 0)),
            pl.BlockSpec((None, kv, hd), lambda b: (b, 0, 0)),
            pl.BlockSpec((None, kv, hd), lambda b: (b, 0, 0)),
            pl.BlockSpec((None, buf, kw), lambda b: (b, 0, 0)),
            pl.BlockSpec((None, buf, kw), lambda b: (b, 0, 0)),
            pl.BlockSpec((None, heads, 1), lambda b: (l, 0, 0)),
        ],
        out_specs=pl.BlockSpec((None, heads, hd), lambda b: (b, 0, 0)),
        out_shape=jax.ShapeDtypeStruct((bs, heads, hd), F32),
        compiler_params=_cparams(("arbitrary",)),
        name="swa_step_attn",
    )(q3, kn3, vn3, ck, cv, sinks.reshape(sinks.shape[0], heads, 1))


def _rope_tables(pos, hd):
    half = hd // 2
    inv = ROPE_THETA ** (-jnp.arange(half, dtype=F32) / half)
    ang = pos.astype(F32)[:, None] * inv[None, :]
    cos, sin = jnp.cos(ang), jnp.sin(ang)
    reps = LANES // hd
    return jnp.tile(jnp.concatenate([cos, cos], axis=1), (1, reps)), jnp.tile(jnp.concatenate([-sin, sin], axis=1), (1, reps))


def _swa(hm, j, w, cache_k, cache_v, dims, past_len):
    n_p, seq, batch, bs, np_rows = dims["n_p"], dims["seq"], dims["batch"], dims["bs"], dims["np_rows"]
    kv, hd = cache_k.shape[3], cache_k.shape[4]
    buf = cache_k.shape[2]
    heads = w["swa_sinks"].shape[1]
    qw, kw = heads * hd, kv * hd

    p2 = _mm(hm, w["swa_w_in"], j, w["swa_b_in"], name="swa_in_proj")
    pos = jnp.concatenate([jnp.tile(jnp.arange(seq), batch), jnp.full((np_rows - n_p,), past_len)])
    cos, sin = _rope_tables(pos, hd)
    rot = _rope(p2, cos, sin, width=qw + kw, hd=hd)

    o_p = _swa_prompt(rot, p2, w["swa_sinks"], j, batch=batch, seq=seq, heads=heads, kv=kv, hd=hd)
    keep = min(SWA_WINDOW, seq)
    def last_rows(a, c0):
        tails = [lax.slice(a, ((b + 1) * seq - keep, c0), ((b + 1) * seq, c0 + kw)) for b in range(batch)]
        return jnp.stack(tails).reshape(batch, keep, kv, hd)

    k_p = last_rows(rot, qw)
    v_p = last_rows(p2, qw + kw)

    q_s = rot[n_p:n_p + bs, :qw].reshape(bs, heads, hd)
    kn = rot[n_p:n_p + bs, qw:].reshape(bs, kv, hd)
    vn = p2[n_p:n_p + bs, qw + kw:].reshape(bs, kv, hd)
    o_s = _swa_step(q_s, kn, vn, cache_k[j].reshape(bs, buf, kw), cache_v[j].reshape(bs, buf, kw),
                    w["swa_sinks"], j, heads=heads, kv=kv, hd=hd)
    k_s = jnp.concatenate([cache_k[j], kn[:, None]], axis=1)[:, 1:]
    v_s = jnp.concatenate([cache_v[j], vn[:, None]], axis=1)[:, 1:]

    o = jnp.concatenate([o_p, o_s.reshape(bs, qw).astype(BF16), jnp.zeros((np_rows - n_p - bs, qw), BF16)], axis=0)
    mix = _mm(o, w["swa_w_out"], j, w["swa_b_out"], name="swa_out_proj")
    return mix, k_p, v_p, k_s, v_s


def _router_kernel(h_ref, w_ref, b_ref, slot_ref, wt_ref, cnt_ref, run_ref, *, n_valid_rows, groups, topk_groups, top_k):
    tr = h_ref.shape[0]
    e = w_ref.shape[1]

    @pl.when(pl.program_id(0) == 0)
    def _():
        run_ref[...] = jnp.zeros(run_ref.shape, F32)

    h = h_ref[...]
    w = w_ref[...]
    wh = w.astype(BF16)
    wl = (w - wh.astype(F32)).astype(BF16)
    logits = jnp.dot(h, wh, preferred_element_type=F32) + jnp.dot(h, wl, preferred_element_type=F32)
    scores = jax.nn.sigmoid(logits)
    sel = scores + b_ref[...]
    lane = lax.broadcasted_iota(I32, (tr, e), 1)
    grp = lane // (e // groups)
    neg = -jnp.inf

    def first_max(v, ids, sentinel):
        m = jnp.max(v, axis=-1, keepdims=True)
        return m, jnp.min(jnp.where(v == m, ids, sentinel), axis=-1, keepdims=True)

    gscore = jnp.zeros((tr, e), F32)
    for gi in range(groups):
        v = jnp.where(grp == gi, sel, neg)
        m1, i1 = first_max(v, lane, e)
        m2 = jnp.max(jnp.where(lane == i1, neg, v), axis=-1, keepdims=True)
        gscore = jnp.where(grp == gi, m1 + m2, gscore)
    gmask = jnp.zeros((tr, e), jnp.bool_)
    for _ in range(topk_groups):
        _, gi = first_max(gscore, grp, groups)
        hit = grp == gi
        gmask = gmask | hit
        gscore = jnp.where(hit, neg, gscore)
    cur = jnp.where(gmask, sel, neg)
    chosen = jnp.zeros((tr, e), jnp.bool_)
    picks = []
    for _ in range(top_k):
        _, ei = first_max(cur, lane, e)
        hit = lane == ei
        picks.append((ei, hit))
        chosen = chosen | hit
        cur = jnp.where(hit, neg, cur)
    wts = jnp.where(chosen, scores, 0.0)
    gates = wts / (jnp.sum(wts, axis=-1, keepdims=True) + 1e-20) * ROUTED_SCALE
    real = (pl.program_id(0) * tr + lax.broadcasted_iota(I32, (tr, e), 0)) < n_valid_rows
    chosen_f = jnp.where(chosen & real, 1.0, 0.0)
    before = lax.broadcasted_iota(I32, (tr, tr), 0) > lax.broadcasted_iota(I32, (tr, tr), 1)
    rank = run_ref[...] + jnp.dot(before.astype(BF16), chosen_f.astype(BF16), preferred_element_type=F32)
    run_ref[...] = run_ref[...] + jnp.sum(chosen_f, axis=0, keepdims=True)
    cnt_ref[...] = run_ref[...].astype(I32)
    real_row = real[:, :1]
    for s, (ei, hit) in enumerate(picks):
        r = jnp.sum(jnp.where(hit, rank, 0.0), axis=-1, keepdims=True).astype(I32)
        slot_ref[:, s:s + 1] = jnp.where(real_row, ei * (1 << RANK_BITS) + r, e * (1 << RANK_BITS))
        wt_ref[:, s:s + 1] = jnp.where(real_row, jnp.sum(jnp.where(hit, gates, 0.0), axis=-1, keepdims=True), 0.0)


def _router(hf, router_w, router_b, l, *, n_valid_rows):
    rows, d = hf.shape
    e = router_w.shape[2]
    k = MOE_TOP_K
    tr = ROW_TILE
    return pl.pallas_call(
        functools.partial(_router_kernel, n_valid_rows=n_valid_rows, groups=MOE_GROUPS,
                          topk_groups=MOE_TOPK_GROUPS, top_k=MOE_TOP_K),
        grid=(rows // tr,),
        in_specs=[
            pl.BlockSpec((tr, d), lambda i: (i, 0)),
            pl.BlockSpec((None, d, e), lambda i: (l, 0, 0)),
            pl.BlockSpec((None, 1, e), lambda i: (l, 0, 0)),
        ],
        out_specs=[pl.BlockSpec((tr, k), lambda i: (i, 0)), pl.BlockSpec((tr, k), lambda i: (i, 0)),
                   pl.BlockSpec((1, e), lambda i: (0, 0))],
        out_shape=[jax.ShapeDtypeStruct((rows, k), I32), jax.ShapeDtypeStruct((rows, k), F32),
                   jax.ShapeDtypeStruct((1, e), I32)],
        scratch_shapes=[pltpu.VMEM((1, e), F32)],
        compiler_params=_cparams(("arbitrary",)),
        name="moe_router",
    )(hf, router_w, router_b.reshape(router_b.shape[0], 1, e))


def _row_copy(src_hbm, dst_vmem, sem, src_row, dst_row):
    return pltpu.make_async_copy(src_hbm.at[src_row], dst_vmem.at[dst_row], sem)


def _slabs_to_rows(x):
    y = jnp.swapaxes(x, 0, 1)
    return [y[s] for s in range(y.shape[0])]


def _moe_gather_kernel(tok_ref, start_ref, nrows_ref, nvalid_ref, x_ref, o_ref, buf_ref, sems, *, tm, sub):
    i = pl.program_id(0)
    nv = nvalid_ref[0]

    def issue(tile, half):
        first = start_ref[tile]
        last = first + nrows_ref[tile] - 1

        def body(r, carry):
            _row_copy(x_ref, buf_ref, sems.at[half], tok_ref[jnp.minimum(first + r, last)], half * tm + r).start()
            return carry

        lax.fori_loop(0, tm, body, 0, unroll=DMA_UNROLL)

    @pl.when((i == 0) & (nv > 0))
    def _():
        issue(0, 0)

    @pl.when(i + 1 < nv)
    def _():
        issue(i + 1, (i + 1) % 2)

    @pl.when(i < nv)
    def _():
        half = i % 2

        def drain(r, carry):
            _row_copy(x_ref, buf_ref, sems.at[half], 0, half * tm + r).wait()
            return carry

        lax.fori_loop(0, tm, drain, 0, unroll=DMA_UNROLL)
        for t0 in range(0, tm, sub):
            cols = _slabs_to_rows(buf_ref[pl.ds(half * tm + t0, sub)])
            for s, col in enumerate(cols):
                o_ref[t0:t0 + sub, s * LANES:(s + 1) * LANES] = col.astype(o_ref.dtype)

    @pl.when(i >= nv)
    def _():
        o_ref[...] = jnp.zeros(o_ref.shape, o_ref.dtype)


def _moe_gather(sorted_tok, tile_start, tile_rows, n_valid_tiles, x_slab, *, n_tiles, d):
    tm = MOE_TILE
    slab = d // LANES
    gs = pltpu.PrefetchScalarGridSpec(
        num_scalar_prefetch=4,
        grid=(n_tiles,),
        in_specs=[pl.BlockSpec(memory_space=pl.ANY)],
        out_specs=pl.BlockSpec((tm, d), lambda i, *_: (i, 0)),
        scratch_shapes=[pltpu.VMEM((2 * tm, slab, LANES), F32), pltpu.SemaphoreType.DMA((2,))],
    )
    return pl.pallas_call(
        functools.partial(_moe_gather_kernel, tm=tm, sub=TRANSPOSE_ROWS),
        grid_spec=gs,
        out_shape=jax.ShapeDtypeStruct((n_tiles * tm, d), BF16),
        compiler_params=_cparams(("arbitrary",), VMEM_LIMIT, bounds_checks=False),
        name="moe_gather",
    )(sorted_tok, tile_start, tile_rows, n_valid_tiles, x_slab)


def _expert_changed(te_ref, i):
    return (i == 0) | (te_ref[i] != te_ref[jnp.maximum(i - 1, 0)])


def _moe_up_kernel(te_ref, nvalid_ref, x_ref, wg_ref, wu_ref, o_ref, wgc_ref, wuc_ref):
    i = pl.program_id(1)

    @pl.when(i < nvalid_ref[0])
    def _():
        @pl.when(_expert_changed(te_ref, i))
        def _():
            wgc_ref[...] = wg_ref[...].astype(BF16)
            wuc_ref[...] = wu_ref[...].astype(BF16)

        x = x_ref[...]
        g = jnp.dot(x, wgc_ref[...], preferred_element_type=F32)
        u = jnp.dot(x, wuc_ref[...], preferred_element_type=F32)
        o_ref[...] = (_silu(g) * u).astype(o_ref.dtype)

    @pl.when(i >= nvalid_ref[0])
    def _():
        o_ref[...] = jnp.zeros(o_ref.shape, o_ref.dtype)


def _moe_up(tile_expert, n_valid_tiles, xs, w_in, l, *, n_tiles):
    tm = MOE_TILE
    d = xs.shape[1]
    de = w_in.shape[3] // 2
    tj = _pick_tile(de, 512, LANES)
    nj = de // tj

    def tile(i, nv):
        return jnp.minimum(i, nv[0] - 1)

    gs = pltpu.PrefetchScalarGridSpec(
        num_scalar_prefetch=2,
        grid=(nj, n_tiles),
        in_specs=[
            pl.BlockSpec((tm, d), lambda j, i, te, nv: (tile(i, nv), 0)),
            pl.BlockSpec((None, None, d, tj), lambda j, i, te, nv: (l, te[i], 0, j)),
            pl.BlockSpec((None, None, d, tj), lambda j, i, te, nv: (l, te[i], 0, nj + j)),
        ],
        out_specs=pl.BlockSpec((tm, tj), lambda j, i, te, nv: (i, j)),
        scratch_shapes=[pltpu.VMEM((d, tj), BF16), pltpu.VMEM((d, tj), BF16)],
    )
    return pl.pallas_call(
        _moe_up_kernel,
        grid_spec=gs,
        out_shape=jax.ShapeDtypeStruct((n_tiles * tm, de), BF16),
        compiler_params=_cparams(("arbitrary", "arbitrary"), VMEM_LIMIT),
        name="moe_up",
    )(tile_expert, n_valid_tiles, xs, w_in, w_in)


def _moe_down_kernel(te_ref, nvalid_ref, h_ref, w_ref, o_ref, wc_ref, *, tm):
    i = pl.program_id(1)

    @pl.when(i < nvalid_ref[0])
    def _():
        @pl.when(_expert_changed(te_ref, i))
        def _():
            wc_ref[...] = w_ref[...].astype(BF16)

        y = jnp.dot(h_ref[...], wc_ref[...], preferred_element_type=F32)
        sub = TRANSPOSE_ROWS
        for t0 in range(0, tm, sub):
            cols = [y[t0:t0 + sub, s * LANES:(s + 1) * LANES] for s in range(y.shape[1] // LANES)]
            o_ref[t0:t0 + sub] = jnp.swapaxes(jnp.stack(cols, axis=0), 0, 1)

    @pl.when(i >= nvalid_ref[0])
    def _():
        o_ref[...] = jnp.zeros(o_ref.shape, o_ref.dtype)


def _moe_down(tile_expert, n_valid_tiles, h, w_out, l, *, n_tiles):
    tm = MOE_TILE
    de, d = w_out.shape[2], w_out.shape[3]
    tn = _pick_tile(d, 2048, LANES)
    nn = d // tn

    def tile(i, nv):
        return jnp.minimum(i, nv[0] - 1)

    gs = pltpu.PrefetchScalarGridSpec(
        num_scalar_prefetch=2,
        grid=(nn, n_tiles),
        in_specs=[
            pl.BlockSpec((tm, de), lambda j, i, te, nv: (tile(i, nv), 0)),
            pl.BlockSpec((None, None, de, tn), lambda j, i, te, nv: (l, te[i], 0, j)),
        ],
        out_specs=pl.BlockSpec((tm, tn // LANES, LANES), lambda j, i, te, nv: (i, j, 0)),
        scratch_shapes=[pltpu.VMEM((de, tn), BF16)],
    )
    return pl.pallas_call(
        functools.partial(_moe_down_kernel, tm=tm),
        grid_spec=gs,
        out_shape=jax.ShapeDtypeStruct((n_tiles * tm, d // LANES, LANES), F32),
        compiler_params=_cparams(("arbitrary", "arbitrary"), VMEM_LIMIT),
        name="moe_down",
    )(tile_expert, n_valid_tiles, h, w_out)


def _moe_combine_kernel(slot_ref, wt_ref, off_ref, ys_ref, sh_ref, o_ref, buf_ref, acc_ref, sems, *, tr, k):
    i = pl.program_id(0)
    n_rows = tr * k

    def issue(step, half):
        def body(r, carry):
            p = slot_ref[step * n_rows + r]
            dest = off_ref[p >> RANK_BITS] + (p & ((1 << RANK_BITS) - 1))
            _row_copy(ys_ref, buf_ref, sems.at[half], dest, half * n_rows + r).start()
            return carry

        lax.fori_loop(0, n_rows, body, 0, unroll=DMA_UNROLL)

    @pl.when(i == 0)
    def _():
        issue(0, 0)

    @pl.when(i + 1 < pl.num_programs(0))
    def _():
        issue(i + 1, (i + 1) % 2)

    half = i % 2

    def drain(r, carry):
        _row_copy(ys_ref, buf_ref, sems.at[half], 0, half * n_rows + r).wait()
        return carry

    lax.fori_loop(0, n_rows, drain, 0, unroll=DMA_UNROLL)

    def token(r, carry):
        row0 = half * n_rows + r * k
        w0 = (i * tr + r) * k
        acc = wt_ref[w0] * buf_ref[row0]
        for e in range(1, k):
            acc = acc + wt_ref[w0 + e] * buf_ref[row0 + e]
        acc_ref[r] = acc
        return carry

    lax.fori_loop(0, tr, token, 0, unroll=2)
    for s, col in enumerate(_slabs_to_rows(acc_ref[...])):
        o_ref[:, s * LANES:(s + 1) * LANES] = sh_ref[:, s * LANES:(s + 1) * LANES] + col


def _moe_combine(slots, wts, row_off, ys, shared, *, k):
    rows, d = shared.shape
    slab = d // LANES
    tr = COMBINE_TILE
    gs = pltpu.PrefetchScalarGridSpec(
        num_scalar_prefetch=3,
        grid=(rows // tr,),
        in_specs=[pl.BlockSpec(memory_space=pl.ANY), pl.BlockSpec((tr, d), lambda i, *_: (i, 0))],
        out_specs=pl.BlockSpec((tr, d), lambda i, *_: (i, 0)),
        scratch_shapes=[pltpu.VMEM((2 * tr * k, slab, LANES), F32), pltpu.VMEM((tr, slab, LANES), F32),
                        pltpu.SemaphoreType.DMA((2,))],
    )
    return pl.pallas_call(
        functools.partial(_moe_combine_kernel, tr=tr, k=k),
        grid_spec=gs,
        out_shape=jax.ShapeDtypeStruct((rows, d), F32),
        compiler_params=_cparams(("arbitrary",), VMEM_LIMIT, bounds_checks=False),
        name="moe_combine",
    )(slots, wts, row_off, ys, shared)


def _moe_plan_kernel(cnt_ref, te_ref, ts_ref, tr_ref, nv_ref, off_ref, *, n_experts, tm, n_tiles):
    def per_expert(e, carry):
        first_tile, first_tok = carry
        cnt = cnt_ref[e]
        nt = (cnt + tm - 1) // tm
        off_ref[e] = first_tile * tm

        def fill(t, c):
            te_ref[first_tile + t] = e
            ts_ref[first_tile + t] = first_tok + t * tm
            tr_ref[first_tile + t] = jnp.minimum(cnt - t * tm, tm)
            return c

        lax.fori_loop(0, nt, fill, 0)
        return first_tile + nt, first_tok + cnt

    nv, _ = lax.fori_loop(0, n_experts, per_expert, (0, 0))
    off_ref[n_experts] = 0
    nv_ref[0] = nv
    last = jnp.maximum(nv - 1, 0)

    def fill_tail(t, c):
        te_ref[t] = te_ref[last]
        ts_ref[t] = ts_ref[last]
        tr_ref[t] = tr_ref[last]
        return c

    lax.fori_loop(nv, n_tiles, fill_tail, 0)


def _moe_plan(counts, *, n_pairs, tm):
    e = counts.shape[0]
    n_tiles = n_pairs // tm + e
    smem = pl.BlockSpec(memory_space=pltpu.SMEM)
    tiles = jax.ShapeDtypeStruct((n_tiles,), I32)
    tile_expert, tile_start, tile_rows, n_valid, row_off = pl.pallas_call(
        functools.partial(_moe_plan_kernel, n_experts=e, tm=tm, n_tiles=n_tiles),
        in_specs=[smem],
        out_specs=[smem] * 5,
        out_shape=[tiles, tiles, tiles, jax.ShapeDtypeStruct((1,), I32), jax.ShapeDtypeStruct((e + 1,), I32)],
        name="moe_plan",
    )(counts)
    return tile_expert, tile_start, tile_rows, n_valid, row_off, n_tiles


def _moe(hf, hf_slab, l, w, *, n_valid_rows):
    rows, d = hf.shape
    k = MOE_TOP_K
    assert rows < (1 << RANK_BITS)
    slots, wts, counts = _router(hf, w["moe_router_w"], w["moe_router_bias"], l, n_valid_rows=n_valid_rows)
    slots = slots.reshape(rows * k)
    sorted_tok = (jnp.argsort(slots >> RANK_BITS, stable=True) // k).astype(I32)
    tile_expert, tile_start, tile_rows, n_valid, row_off, n_tiles = _moe_plan(
        counts.reshape(-1), n_pairs=rows * k, tm=MOE_TILE)
    xs = _moe_gather(sorted_tok, tile_start, tile_rows, n_valid, hf_slab, n_tiles=n_tiles, d=d)
    hmid = _moe_up(tile_expert, n_valid, xs, w["moe_w_in"], l, n_tiles=n_tiles)
    ys = _moe_down(tile_expert, n_valid, hmid, w["moe_w_out"], l, n_tiles=n_tiles)
    shared = _mm(_glu(hf, w["moe_sh_in"], l, name="moe_shared_up"), w["moe_sh_out"], l, name="moe_shared_down")
    return _moe_combine(slots, wts.reshape(rows * k), row_off, ys, shared, k=k)


def kernel(x_prompt, x_sample, state_dn_conv, state_dn_rec, cache_swa_k, cache_swa_v, c_prompt, c_sample, ada_w, ada_b, norm_mix_g, norm_ffn_g, final_norm_g, dn_w_in, dn_conv_w, dn_a_log, dn_dt_bias, dn_norm_g, dn_w_out, swa_w_in, swa_b_in, swa_sinks, swa_w_out, swa_b_out, moe_router_w, moe_router_bias, moe_w_in, moe_w_out, moe_sh_in, moe_sh_out):
    w = dict(dn_w_in=dn_w_in, dn_conv_w=dn_conv_w, dn_a_log=dn_a_log, dn_dt_bias=dn_dt_bias, dn_norm_g=dn_norm_g,
             dn_w_out=dn_w_out, swa_w_in=swa_w_in, swa_b_in=swa_b_in, swa_sinks=swa_sinks, swa_w_out=swa_w_out,
             swa_b_out=swa_b_out, moe_router_w=moe_router_w, moe_router_bias=moe_router_bias, moe_w_in=moe_w_in,
             moe_w_out=moe_w_out, moe_sh_in=moe_sh_in, moe_sh_out=moe_sh_out)
    batch, seq, d = x_prompt.shape
    bs, dec_seq, _ = x_sample.shape
    assert dec_seq == 1 and seq % ROW_TILE == 0 and seq % SWA_WINDOW == 0 and seq % DN_CHUNK == 0
    depth = ada_w.shape[0]
    n_p = batch * seq
    assert n_p % bs == 0 and bs % SUBLANES == 0
    np_rows = _round_up(n_p + bs, ROW_TILE)
    rs = np_rows - n_p
    dims = dict(n_p=n_p, seq=seq, batch=batch, bs=bs, np_rows=np_rows)
    past_len = PAST_LEN

    x = jnp.concatenate([x_prompt.reshape(n_p, d), x_sample.reshape(bs, d), jnp.zeros((rs - bs, d), F32)], axis=0)
    cp = _round_up(batch + bs, 16)
    c_all = jnp.concatenate([c_prompt, c_sample, jnp.zeros((cp - batch - bs, d), F32)], axis=0)

    norm_g = jnp.stack([norm_mix_g, norm_ffn_g], axis=1).reshape(2 * depth, d)
    final_g = final_norm_g.reshape(1, d)

    def modulation(l):
        mod = _mm(c_all, ada_w, l, ada_b, lhs_silu=True, name="ada_mod")
        modp = mod[:batch].reshape(batch, 6, 1, d)
        modr = jnp.pad(mod[batch:batch + bs], ((0, rs - bs), (0, 0))).reshape(rs, 6, d).transpose(1, 0, 2)
        return modp, modr

    rn = functools.partial(_rownorm, n_prompt_rows=n_p, rows_per_seq=seq)
    p_conv, p_rec, p_k, p_v, s_conv, s_rec, s_k, s_v = [], [], [], [], [], [], [], []
    modp, modr = modulation(0)
    (hm,) = rn(x, norm_g, 0, modp, modr, mod=(MOD_SC_M, MOD_SH_M), name="norm_mix")
    y = None
    for l in range(depth):
        j = l // 2
        if l % 2 == 0:
            mix, cv_p, st_p, cv_s, st_s = _deltanet(hm, j, w, state_dn_conv, state_dn_rec, dims)
            p_conv.append(cv_p), p_rec.append(st_p), s_conv.append(cv_s), s_rec.append(st_s)
        else:
            mix, kk_p, vv_p, kk_s, vv_s = _swa(hm, j, w, cache_swa_k, cache_swa_v, dims, past_len)
            p_k.append(kk_p), p_v.append(vv_p), s_k.append(kk_s), s_v.append(vv_s)
        x, hf, hf_slab = rn(x, norm_g, 2 * l + 1, modp, modr, resid=(mix, MOD_GT_M), mod=(MOD_SC_F, MOD_SH_F),
                            out_x=True, out_slab=True, name="resid_norm_ffn")
        moe = _moe(hf, hf_slab, l, w, n_valid_rows=n_p + bs)
        if l + 1 < depth:
            gt_mods = (modp, modr)
            modp, modr = modulation(l + 1)
            x, hm = _resid_then_norm(x, moe, gt_mods, norm_g, 2 * (l + 1), (modp, modr), n_p, seq)
        else:
            (y,) = rn(x, final_g, 0, modp, modr, resid=(moe, MOD_GT_F), h_dtype=F32, name="final_norm")

    y_prompt = y[:n_p].reshape(batch, seq, d)
    y_sample = y[n_p:n_p + bs].reshape(bs, 1, d)
    return (y_prompt, y_sample, jnp.stack(p_conv), jnp.stack(p_rec), jnp.stack(p_k), jnp.stack(p_v),
            jnp.stack(s_conv), jnp.stack(s_rec), jnp.stack(s_k), jnp.stack(s_v))


def _resid_then_norm(x, y, gt_mods, norm_g, g_row, next_mods, n_p, seq):
    modp, modr = gt_mods
    nmodp, nmodr = next_mods
    modp2 = jnp.concatenate([modp[:, MOD_GT_F:MOD_GT_F + 1], nmodp[:, MOD_SC_M:MOD_SC_M + 1], nmodp[:, MOD_SH_M:MOD_SH_M + 1]], axis=1)
    modr2 = jnp.stack([modr[MOD_GT_F], nmodr[MOD_SC_M], nmodr[MOD_SH_M]], axis=0)
    return _rownorm(x, norm_g, g_row, modp2, modr2, n_prompt_rows=n_p, rows_per_seq=seq, resid=(y, 0), mod=(1, 2),
                    out_x=True, name="resid_norm_mix")
```
